```python
import jax
import jax.numpy as jnp
from jax import lax
import numpy as np

D_MODEL = 4096
BATCH = 1
SEQ = 16384
DEPTH = 2
DEC_BATCH = 16
DEC_SEQ = 16
PAST_LEN = 2048

CHUNK = 64
N_A = DEPTH // 2
N_B = DEPTH - N_A
D_MIX = D_MODEL
D_MEMQ = D_MIX // 4
MEM_HEADS = 4
MEM_HD = D_MEMQ // MEM_HEADS
N_MEM = 256
D_POOL = D_MIX - D_MEMQ
POOL_WINDOWS = (2, 4, 8, 16)
POOL_GROUP = D_POOL // len(POOL_WINDOWS)
POOL_STATE = max(POOL_WINDOWS) - 1
HD_FOX = 128
H_FOX = (D_MIX - D_MEMQ) // HD_FOX
D_FOX = H_FOX * HD_FOX
D_FF = 11008
Q_BLOCK = 128
EPS = 1e-6
N_NORMS = 6
FORGET_BIAS = 2.0

kernel_name = "yoco_pool_fox_macaron_stream_step"


def rmsnorm(x, g):
    xf = x.astype(jnp.float32)
    y = xf * lax.rsqrt(jnp.mean(xf * xf, axis=-1, keepdims=True) + EPS)
    return (y * g.astype(jnp.float32)).astype(x.dtype)


def swiglu(h, wg, wu, wd):
    return (jax.nn.silu(h @ wg) * (h @ wu)) @ wd


def pool_mixer(u, u_prev, pos, pool_w, pool_scale):
    b, n, _ = u.shape
    ext = jnp.concatenate([u_prev.astype(u.dtype), u], axis=1).astype(jnp.float32)
    cs = jnp.cumsum(jnp.pad(ext, ((0, 0), (1, 0), (0, 0))), axis=1)
    end = cs[:, POOL_STATE + 1:]
    outs = []
    for gi, w in enumerate(POOL_WINDOWS):
        sl = slice(gi * POOL_GROUP, (gi + 1) * POOL_GROUP)
        start = cs[:, POOL_STATE + 1 - w:POOL_STATE + 1 - w + n, sl]
        cnt = jnp.minimum(pos + 1, w).astype(jnp.float32)[None, :, None]
        outs.append((end[..., sl] - start) / cnt)
    pooled = jnp.stack(outs, axis=2)
    diff = pooled - u.reshape(b, n, len(POOL_WINDOWS), POOL_GROUP).astype(jnp.float32)
    y = jnp.einsum('bngc,gcd->bngd', diff.astype(pool_w.dtype), pool_w).reshape(b, n, D_POOL)
    y = (y * pool_scale).astype(u.dtype)
    return y, ext[:, -POOL_STATE:].astype(u.dtype)


def fox_block(q, fq, pos_q, k, v, fk, pos_k):
    s = jnp.einsum('bqhd,bkhd->bhqk', q, k).astype(jnp.float32) * (HD_FOX ** -0.5)
    s = s + jnp.transpose(fq, (0, 2, 1))[..., :, None] - jnp.transpose(fk, (0, 2, 1))[..., None, :]
    s = jnp.where(pos_k[None, :] <= pos_q[:, None], s, -jnp.inf)
    p = jax.nn.softmax(s, axis=-1)
    return jnp.einsum('bhqk,bkhd->bqhd', p.astype(v.dtype), v)


def fox_attend(q, fq, pos_q, k, v, fk, pos_k):
    b, n, h, d = q.shape
    if n <= Q_BLOCK:
        return fox_block(q, fq, pos_q, k, v, fk, pos_k).reshape(b, n, h * d)
    nb = n // Q_BLOCK
    qb = q.reshape(b, nb, Q_BLOCK, h, d).transpose(1, 0, 2, 3, 4)
    fb = fq.reshape(b, nb, Q_BLOCK, h).transpose(1, 0, 2, 3)
    pb = pos_q.reshape(nb, Q_BLOCK)
    out = lax.map(lambda a: fox_block(a[0], a[1], a[2], k, v, fk, pos_k), (qb, fb, pb))
    return out.transpose(1, 0, 2, 3, 4).reshape(b, n, h * d)


def mem_attend(q, mk, mv):
    b, n = q.shape[:2]
    s = jnp.einsum('bqhd,bkhd->bhqk', q, mk).astype(jnp.float32) * (MEM_HD ** -0.5)
    p = jax.nn.softmax(s, axis=-1)
    return jnp.einsum('bhqk,bkhd->bqhd', p.astype(mv.dtype), mv).reshape(b, n, D_MEMQ)


def memory_kv(mem, g_mem, w_mem_kv):
    b, m, _ = mem.shape
    ks, vs = [], []
    for l in range(DEPTH):
        kv = rmsnorm(mem, g_mem[l]) @ w_mem_kv[l]
        ks.append(kv[..., :D_MEMQ].reshape(b, m, MEM_HEADS, MEM_HD))
        vs.append(kv[..., D_MEMQ:].reshape(b, m, MEM_HEADS, MEM_HD))
    return jnp.stack(ks), jnp.stack(vs)


def trunk(x, pool_prev, mem_k, mem_v, k_past, v_past, logf_past,
          norm_g, w_ffn_gate, w_ffn_up, w_ffn_down, w_in_a, pool_w, pool_scale, w_out_a,
          w_in_b, w_out_b, g_kv, w_kv, w_f, b_f):
    b, n, _ = x.shape
    p = k_past.shape[1]
    pos = p + jnp.arange(n, dtype=jnp.int32)
    pos_k = jnp.arange(p + n, dtype=jnp.int32)
    new_pool = []
    k_new = v_new = logf_new = None
    k_all = v_all = f_all = None
    for l in range(DEPTH):
        g = norm_g[l]
        x = x + 0.5 * rmsnorm(swiglu(rmsnorm(x, g[0]), w_ffn_gate[l, 0], w_ffn_up[l, 0], w_ffn_down[l, 0]), g[1])
        h = rmsnorm(x, g[2])
        if l < N_A:
            proj = h @ w_in_a[l]
            main, u_last = pool_mixer(proj[..., :D_POOL], pool_prev[l], pos, pool_w[l], pool_scale[l])
            new_pool.append(u_last)
            w_out = w_out_a[l]
        else:
            j = l - N_A
            proj = h @ w_in_b[j]
            q = proj[..., :D_FOX].reshape(b, n, H_FOX, HD_FOX)
            main = fox_attend(q, f_all[:, p:], pos, k_all, v_all, f_all, pos_k)
            w_out = w_out_b[j]
        qm = proj[..., D_MIX - D_MEMQ:].reshape(b, n, MEM_HEADS, MEM_HD)
        mixed = jnp.concatenate([main, mem_attend(qm, mem_k[l], mem_v[l])], axis=-1) @ w_out
        x = x + rmsnorm(mixed, g[3])
        x = x + 0.5 * rmsnorm(swiglu(rmsnorm(x, g[4]), w_ffn_gate[l, 1], w_ffn_up[l, 1], w_ffn_down[l, 1]), g[5])
        if l == N_A - 1:
            hk = rmsnorm(x, g_kv)
            kv = hk @ w_kv
            k_new = kv[..., :D_FOX].reshape(b, n, H_FOX, HD_FOX)
            v_new = kv[..., D_FOX:].reshape(b, n, H_FOX, HD_FOX)
            logf_new = jax.nn.log_sigmoid((hk @ w_f + b_f).astype(jnp.float32))
            k_all = jnp.concatenate([k_past.astype(k_new.dtype), k_new], axis=1)
            v_all = jnp.concatenate([v_past.astype(v_new.dtype), v_new], axis=1)
            f_all = jnp.cumsum(jnp.concatenate([logf_past.astype(jnp.float32), logf_new], axis=1), axis=1)
    return x, jnp.stack(new_pool), k_new, v_new, logf_new


def setup_inputs(seed: int = 0) -> dict:
    key = jax.random.key(seed)
    ks = jax.random.split(key, 26)

    def nrm(k, shape, scale):
        return scale * jax.random.normal(k, shape, jnp.float32)

    return {
        "x_prompt": nrm(ks[0], (BATCH, SEQ, D_MODEL), 1.0),
        "x_sample": nrm(ks[1], (DEC_BATCH, DEC_SEQ, D_MODEL), 1.0),
        "mem_prompt": nrm(ks[2], (BATCH, N_MEM, D_MODEL), 1.0),
        "cache_fox_k": nrm(ks[3], (DEC_BATCH, PAST_LEN, H_FOX, HD_FOX), 1.0),
        "cache_fox_v": nrm(ks[4], (DEC_BATCH, PAST_LEN, H_FOX, HD_FOX), 1.0),
        "cache_fox_logf": jax.nn.log_sigmoid(FORGET_BIAS + nrm(ks[5], (DEC_BATCH, PAST_LEN, H_FOX), 0.5)),
        "cache_mem_k": nrm(ks[6], (DEPTH, DEC_BATCH, N_MEM, MEM_HEADS, MEM_HD), 1.0),
        "cache_mem_v": nrm(ks[7], (DEPTH, DEC_BATCH, N_MEM, MEM_HEADS, MEM_HD), 1.0),
        "state_pool": nrm(ks[8], (N_A, DEC_BATCH, POOL_STATE, D_POOL), 1.0),
        "norm_g": 1.0 + nrm(ks[9], (DEPTH, N_NORMS, D_MODEL), 0.05),
        "w_ffn_gate": nrm(ks[10], (DEPTH, 2, D_MODEL, D_FF), D_MODEL ** -0.5),
        "w_ffn_up": nrm(ks[11], (DEPTH, 2, D_MODEL, D_FF), D_MODEL ** -0.5),
        "w_ffn_down": nrm(ks[12], (DEPTH, 2, D_FF, D_MODEL), D_FF ** -0.5),
        "w_in_a": nrm(ks[13], (N_A, D_MODEL, D_MIX), D_MODEL ** -0.5),
        "pool_w": nrm(ks[14], (N_A, len(POOL_WINDOWS), POOL_GROUP, POOL_GROUP), POOL_GROUP ** -0.5),
        "pool_scale": 1.0 + nrm(ks[15], (N_A, D_POOL), 0.05),
        "w_out_a": nrm(ks[16], (N_A, D_MIX, D_MODEL), D_MIX ** -0.5),
        "w_in_b": nrm(ks[17], (N_B, D_MODEL, D_MIX), D_MODEL ** -0.5),
        "w_out_b": nrm(ks[18], (N_B, D_MIX, D_MODEL), D_MIX ** -0.5),
        "g_kv": 1.0 + nrm(ks[19], (D_MODEL,), 0.05),
        "w_kv": nrm(ks[20], (D_MODEL, 2 * D_FOX), D_MODEL ** -0.5),
        "w_f": nrm(ks[21], (D_MODEL, H_FOX), 0.5 * D_MODEL ** -0.5),
        "b_f": FORGET_BIAS + nrm(ks[22], (H_FOX,), 0.5),
        "g_mem": 1.0 + nrm(ks[23], (DEPTH, D_MODEL), 0.05),
        "w_mem_kv": nrm(ks[24], (DEPTH, D_MODEL, 2 * D_MEMQ), D_MODEL ** -0.5),
    }


def reference(x_prompt, x_sample, mem_prompt, cache_fox_k, cache_fox_v, cache_fox_logf,
              cache_mem_k, cache_mem_v, state_pool, norm_g, w_ffn_gate, w_ffn_up, w_ffn_down,
              w_in_a, pool_w, pool_scale, w_out_a, w_in_b, w_out_b, g_kv, w_kv, w_f, b_f,
              g_mem, w_mem_kv):
    b = x_prompt.shape[0]
    mem_k_p, mem_v_p = memory_kv(mem_prompt, g_mem, w_mem_kv)
    y_p, pool_p, k_p, v_p, logf_p = trunk(
        x_prompt,
        jnp.zeros((N_A, b, POOL_STATE, D_POOL), x_prompt.dtype),
        mem_k_p, mem_v_p,
        jnp.zeros((b, 0, H_FOX, HD_FOX), x_prompt.dtype),
        jnp.zeros((b, 0, H_FOX, HD_FOX), x_prompt.dtype),
        jnp.zeros((b, 0, H_FOX), jnp.float32),
        norm_g, w_ffn_gate, w_ffn_up, w_ffn_down, w_in_a, pool_w, pool_scale, w_out_a,
        w_in_b, w_out_b, g_kv, w_kv, w_f, b_f)
    y_s, pool_s, k_s, v_s, logf_s = trunk(
        x_sample, state_pool, cache_mem_k, cache_mem_v, cache_fox_k, cache_fox_v, cache_fox_logf,
        norm_g, w_ffn_gate, w_ffn_up, w_ffn_down, w_in_a, pool_w, pool_scale, w_out_a,
        w_in_b, w_out_b, g_kv, w_kv, w_f, b_f)
    return (y_p, y_s, k_p, v_p, logf_p, mem_k_p, mem_v_p, pool_p, k_s, v_s, logf_s, pool_s)
```

```python
import functools

import jax
import jax.numpy as jnp
from jax import lax
from jax.experimental import pallas as pl
from jax.experimental.pallas import tpu as pltpu

EPS = 1e-6
POOL_WINDOWS = (2, 4, 8, 16)
POOL_STATE = max(POOL_WINDOWS) - 1
HALO = POOL_STATE + 1
LANE = 128
FF_ALIGN = 1024
MASK_VALUE = -1e30
VMEM_CAP = 60 * 1024 * 1024
F32 = jnp.float32
BF16 = jnp.bfloat16


def _tile(n, pref):
    if n <= pref:
        return n
    t = pref
    while n % t:
        t //= 2
    return t


def _params(sem, est_bytes):
    limit = int(min(VMEM_CAP, max(32 * 1024 * 1024, est_bytes * 5 // 4 + (4 << 20))))
    return pltpu.CompilerParams(dimension_semantics=sem, vmem_limit_bytes=limit)


def _rms_normalize(xf):
    return xf * lax.rsqrt(jnp.mean(xf * xf, axis=-1, keepdims=True) + EPS)


def _rms_cast_kernel(x_ref, g_ref, o_ref):
    o_ref[...] = (_rms_normalize(x_ref[...]) * g_ref[...]).astype(o_ref.dtype)


def rms_cast(x, g):
    m, d = x.shape
    tm = _tile(m, 512)
    return pl.pallas_call(
        _rms_cast_kernel,
        out_shape=jax.ShapeDtypeStruct((m, d), BF16),
        grid=(m // tm,),
        in_specs=[pl.BlockSpec((tm, d), lambda i: (i, 0)),
                  pl.BlockSpec((1, d), lambda i: (0, 0))],
        out_specs=pl.BlockSpec((tm, d), lambda i: (i, 0)),
        compiler_params=_params(("parallel",), 2 * tm * d * 6),
        name="rms_cast",
    )(x, g.reshape(1, d).astype(F32))


def _mm_kernel(h_ref, w_ref, *o_refs, scale):
    acc = jnp.dot(h_ref[...], w_ref[...], preferred_element_type=F32)
    if scale is not None:
        acc = acc * scale
    for o_ref in o_refs:
        o_ref[...] = acc.astype(o_ref.dtype)


def mm(h, w, out_dtypes, scale=None, tm_pref=1024, tn_pref=1024):
    m, k = h.shape
    n = w.shape[1]
    tm, tn = _tile(m, tm_pref), _tile(n, tn_pref)
    est = 2 * (tm * k * 2 + k * tn * 2) + tm * tn * 4 + sum(
        2 * tm * tn * jnp.dtype(t).itemsize for t in out_dtypes)
    outs = pl.pallas_call(
        functools.partial(_mm_kernel, scale=scale),
        out_shape=[jax.ShapeDtypeStruct((m, n), t) for t in out_dtypes],
        grid=(m // tm, n // tn),
        in_specs=[pl.BlockSpec((tm, k), lambda i, j: (i, 0)),
                  pl.BlockSpec((k, tn), lambda i, j: (0, j))],
        out_specs=[pl.BlockSpec((tm, tn), lambda i, j: (i, j)) for _ in out_dtypes],
        compiler_params=_params(("parallel", "parallel"), est),
        name="mm",
    )(h, w)
    return outs


def _ffn_up_kernel(h_ref, wg_ref, wu_ref, o_ref):
    h = h_ref[...]
    g = jnp.dot(h, wg_ref[...], preferred_element_type=F32)
    u = jnp.dot(h, wu_ref[...], preferred_element_type=F32)
    o_ref[...] = (g * jax.nn.sigmoid(g) * u).astype(o_ref.dtype)


def ffn_up(h, wg, wu):
    m, k = h.shape
    n = wg.shape[1]
    tm, tn = _tile(m, 1024), _tile(n, 512)
    est = 2 * (tm * k * 2 + 2 * k * tn * 2 + tm * tn * 2) + 3 * tm * tn * 4
    return pl.pallas_call(
        _ffn_up_kernel,
        out_shape=jax.ShapeDtypeStruct((m, n), BF16),
        grid=(m // tm, n // tn),
        in_specs=[pl.BlockSpec((tm, k), lambda i, j: (i, 0)),
                  pl.BlockSpec((k, tn), lambda i, j: (0, j)),
                  pl.BlockSpec((k, tn), lambda i, j: (0, j))],
        out_specs=pl.BlockSpec((tm, tn), lambda i, j: (i, j)),
        compiler_params=_params(("parallel", "parallel"), est),
        name="ffn_up",
    )(h, wg, wu)


def _mm_kacc_kernel(a_ref, w_ref, o_ref):
    part = jnp.dot(a_ref[...], w_ref[...], preferred_element_type=F32)

    @pl.when(pl.program_id(2) == 0)
    def _():
        o_ref[...] = part

    @pl.when(pl.program_id(2) > 0)
    def _():
        o_ref[...] += part


def mm_kacc(a, w):
    m, k = a.shape
    n = w.shape[1]
    tm, tn, tk = _tile(m, 1024), _tile(n, 2048), _tile(k, 1024)
    est = 2 * (tm * tk * 2 + tk * tn * 2 + tm * tn * 4) + tm * tn * 4
    return pl.pallas_call(
        _mm_kacc_kernel,
        out_shape=jax.ShapeDtypeStruct((m, n), F32),
        grid=(m // tm, n // tn, k // tk),
        in_specs=[pl.BlockSpec((tm, tk), lambda i, j, kk: (i, kk)),
                  pl.BlockSpec((tk, tn), lambda i, j, kk: (kk, j))],
        out_specs=pl.BlockSpec((tm, tn), lambda i, j, kk: (i, j)),
        compiler_params=_params(("parallel", "parallel", "arbitrary"), est),
        name="mm_kacc",
    )(a, w)


def _resid_norm_kernel(x_ref, y_ref, gp_ref, *rest, coef, n_next):
    gn_refs, xo_ref, h_refs = rest[:n_next], rest[n_next], rest[n_next + 1:]
    xn = x_ref[...] + coef * (_rms_normalize(y_ref[...]) * gp_ref[...])
    xo_ref[...] = xn
    if n_next:
        r = _rms_normalize(xn)
        for gn_ref, h_ref in zip(gn_refs, h_refs):
            h_ref[...] = (r * gn_ref[...]).astype(h_ref.dtype)


def resid_norm(x, y, g_post, coef, g_next):
    m, d = x.shape
    tm = _tile(m, 256)
    n_next = len(g_next)
    row = pl.BlockSpec((tm, d), lambda i: (i, 0))
    gain = pl.BlockSpec((1, d), lambda i: (0, 0))
    outs = pl.pallas_call(
        functools.partial(_resid_norm_kernel, coef=coef, n_next=n_next),
        out_shape=[jax.ShapeDtypeStruct((m, d), F32)]
        + [jax.ShapeDtypeStruct((m, d), BF16) for _ in range(n_next)],
        grid=(m // tm,),
        in_specs=[row, row, gain] + [gain] * n_next,
        out_specs=[row] + [row] * n_next,
        compiler_params=_params(("parallel",), 2 * tm * d * (12 + 2 * n_next) + 2 * tm * d * 4),
        name="resid_norm",
    )(x, y, g_post.reshape(1, d).astype(F32), *[g.reshape(1, d).astype(F32) for g in g_next])
    return outs[0], outs[1:]


def _pool_kernel(u_ref, halo_ref, prev_ref, w_ref, sc_ref, o_ref, ext_ref, *, tm, pos0, group):
    i = pl.program_id(1)

    @pl.when(i == 0)
    def _():
        ext_ref[0:HALO, :] = prev_ref[0]

    @pl.when(i > 0)
    def _():
        ext_ref[0:HALO, :] = halo_ref[0]

    ext_ref[HALO:HALO + tm, :] = u_ref[0]
    pos = pos0 + i * tm + lax.broadcasted_iota(jnp.int32, (tm, 1), 0)
    for gi, win in enumerate(POOL_WINDOWS):
        sl = slice(gi * group, (gi + 1) * group)
        ug = ext_ref[HALO:HALO + tm, sl]
        wsum = ug
        for j in range(1, win):
            wsum = wsum + ext_ref[HALO - j:HALO - j + tm, sl]
        cnt = jnp.minimum(pos + 1, win).astype(F32)
        diff = wsum / cnt - ug
        y = jnp.dot(diff.astype(BF16), w_ref[gi], preferred_element_type=F32)
        o_ref[0, :, sl] = (y * sc_ref[:, sl]).astype(o_ref.dtype)


def pool_mixer(u, prev, pos0, pool_w, pool_scale):
    b, n, c = u.shape
    group = c // len(POOL_WINDOWS)
    tm = _tile(n, 256)
    steps = tm // HALO
    est = 2 * (tm * c * 4 + 2 * HALO * c * 4 + pool_w.size * 2 + tm * c * 2) + (tm + HALO) * c * 4 \
        + 4 * tm * group * 4
    return pl.pallas_call(
        functools.partial(_pool_kernel, tm=tm, pos0=pos0, group=group),
        out_shape=jax.ShapeDtypeStruct((b, n, c), BF16),
        grid=(b, n // tm),
        in_specs=[pl.BlockSpec((1, tm, c), lambda bb, i: (bb, i, 0)),
                  pl.BlockSpec((1, HALO, c), lambda bb, i: (bb, jnp.maximum(i * steps - 1, 0), 0)),
                  pl.BlockSpec((1, HALO, c), lambda bb, i: (bb, 0, 0)),
                  pl.BlockSpec(pool_w.shape, lambda bb, i: (0, 0, 0)),
                  pl.BlockSpec((1, c), lambda bb, i: (0, 0))],
        out_specs=pl.BlockSpec((1, tm, c), lambda bb, i: (bb, i, 0)),
        scratch_shapes=[pltpu.VMEM((tm + HALO, c), F32)],
        compiler_params=_params(("parallel", "parallel"), est),
        name="pool_mixer",
    )(u, u, prev, pool_w, pool_scale.reshape(1, c).astype(F32))


def _mem_attn_kernel(q_ref, k_ref, v_ref, o_ref, *, heads, hd):
    scale = hd ** -0.5
    for hh in range(heads):
        sl = slice(hh * hd, (hh + 1) * hd)
        q = q_ref[0, :, sl]
        k = k_ref[0, :, sl].astype(BF16)
        v = v_ref[0, :, sl].astype(BF16)
        s = lax.dot_general(q, k, (((1,), (1,)), ((), ())), preferred_element_type=F32) * scale
        p = jnp.exp(s - jnp.max(s, axis=-1, keepdims=True))
        p = p / jnp.sum(p, axis=-1, keepdims=True)
        o_ref[0, :, sl] = jnp.dot(p.astype(BF16), v, preferred_element_type=F32).astype(o_ref.dtype)


def mem_attend(qm, mk, mv, heads):
    b, n, dq = qm.shape
    nm = mk.shape[1]
    tm = _tile(n, 512)
    est = 2 * (2 * tm * dq * 2 + 2 * nm * dq * 4) + 4 * tm * nm * 4
    return pl.pallas_call(
        functools.partial(_mem_attn_kernel, heads=heads, hd=dq // heads),
        out_shape=jax.ShapeDtypeStruct((b, n, dq), BF16),
        grid=(b, n // tm),
        in_specs=[pl.BlockSpec((1, tm, dq), lambda bb, i: (bb, i, 0)),
                  pl.BlockSpec((1, nm, dq), lambda bb, i: (bb, 0, 0)),
                  pl.BlockSpec((1, nm, dq), lambda bb, i: (bb, 0, 0))],
        out_specs=pl.BlockSpec((1, tm, dq), lambda bb, i: (bb, i, 0)),
        compiler_params=_params(("parallel", "parallel"), est),
        name="mem_attend",
    )(qm, mk, mv)


def _logf_kernel(h_ref, w_ref, b_ref, o_ref):
    z = jnp.dot(h_ref[...], w_ref[...], preferred_element_type=F32) + b_ref[...]
    o_ref[...] = jnp.minimum(z, 0.0) - jnp.log1p(jnp.exp(-jnp.abs(z)))


def log_forget(hk, w_f, b_f):
    m, k = hk.shape
    nh = w_f.shape[1]
    wp = jnp.pad(w_f, ((0, 0), (0, LANE - nh))).astype(BF16)
    bp = jnp.pad(b_f, (0, LANE - nh)).reshape(1, LANE).astype(F32)
    tm = _tile(m, 1024)
    out = pl.pallas_call(
        _logf_kernel,
        out_shape=jax.ShapeDtypeStruct((m, LANE), F32),
        grid=(m // tm,),
        in_specs=[pl.BlockSpec((tm, k), lambda i: (i, 0)),
                  pl.BlockSpec((k, LANE), lambda i: (0, 0)),
                  pl.BlockSpec((1, LANE), lambda i: (0, 0))],
        out_specs=pl.BlockSpec((tm, LANE), lambda i: (i, 0)),
        compiler_params=_params(("parallel",), 2 * (tm * k * 2 + k * LANE * 2 + tm * LANE * 4)),
        name="log_forget",
    )(hk, wp, bp)
    return out[:, :nh]


def _cumsum_kernel(x_ref, o_ref, carry_ref, *, tb):
    @pl.when(pl.program_id(1) == 0)
    def _():
        carry_ref[...] = jnp.zeros_like(carry_ref)

    x = x_ref[0]
    row = lax.broadcasted_iota(jnp.int32, x.shape, 0)
    d = 1
    while d < tb:
        x = x + jnp.where(row >= d, pltpu.roll(x, d, 0), 0.0)
        d *= 2
    x = x + carry_ref[0:1, :]
    o_ref[0] = x
    carry_ref[...] = jnp.broadcast_to(x[tb - 1:tb, :], carry_ref.shape)


def cumsum_time(logf):
    b, t, nh = logf.shape
    tb = 256
    tp = -(-t // tb) * tb
    x = jnp.pad(logf, ((0, 0), (0, tp - t), (0, LANE - nh)))
    out = pl.pallas_call(
        functools.partial(_cumsum_kernel, tb=tb),
        out_shape=jax.ShapeDtypeStruct((b, tp, LANE), F32),
        grid=(b, tp // tb),
        in_specs=[pl.BlockSpec((1, tb, LANE), lambda bb, i: (bb, i, 0))],
        out_specs=pl.BlockSpec((1, tb, LANE), lambda bb, i: (bb, i, 0)),
        scratch_shapes=[pltpu.VMEM((8, LANE), F32)],
        compiler_params=_params(("parallel", "arbitrary"), 8 * tb * LANE * 4),
        name="cumsum_time",
    )(x)
    return out[:, :t, :nh]


def _softmax_step(s, v, m_ref, l_ref, acc_ref):
    m_prev = m_ref[...]
    m_new = jnp.maximum(m_prev, jnp.max(s, axis=-1, keepdims=True))
    alpha = jnp.exp(m_prev - m_new)
    p = jnp.exp(s - m_new)
    l_ref[...] = alpha * l_ref[...] + jnp.sum(p, axis=-1, keepdims=True)
    acc_ref[...] = alpha * acc_ref[...] + jnp.dot(p.astype(BF16), v, preferred_element_type=F32)
    m_ref[...] = m_new


def _fox_prompt_kernel(q_ref, k_ref, v_ref, f_ref, o_ref, m_ref, l_ref, acc_ref, *, tq):
    qi = pl.program_id(1)
    q = q_ref[...]
    m_ref[...] = jnp.full_like(m_ref, MASK_VALUE)
    l_ref[...] = jnp.zeros_like(l_ref)
    acc_ref[...] = jnp.zeros_like(acc_ref)

    def logits(kb):
        off = pl.multiple_of(kb * tq, tq)
        k = k_ref[pl.ds(off, tq), :]
        s = lax.dot_general(q, k, (((1,), (1,)), ((), ())), preferred_element_type=F32)
        return s - f_ref[0, kb], v_ref[pl.ds(off, tq), :]

    def body(kb, carry):
        s, v = logits(kb)
        _softmax_step(s, v, m_ref, l_ref, acc_ref)
        return carry

    lax.fori_loop(0, qi, body, 0)
    s, v = logits(qi)
    row = lax.broadcasted_iota(jnp.int32, s.shape, 0)
    col = lax.broadcasted_iota(jnp.int32, s.shape, 1)
    _softmax_step(jnp.where(col <= row, s, MASK_VALUE), v, m_ref, l_ref, acc_ref)
    o_ref[...] = (acc_ref[...] / l_ref[...]).astype(o_ref.dtype)


def fox_prompt(q, k, v, f_cum, hd):
    n, dm = q.shape
    nh = dm // hd
    tq = _tile(n, 512)
    f_t = f_cum.T.reshape(nh, n // tq, 1, tq)
    est = 2 * (2 * tq * hd * 2 + 2 * n * hd * 2 + (n // tq) * 8 * tq * 4) + 6 * tq * tq * 4
    return pl.pallas_call(
        functools.partial(_fox_prompt_kernel, tq=tq),
        out_shape=jax.ShapeDtypeStruct((n, dm), BF16),
        grid=(nh, n // tq),
        in_specs=[pl.BlockSpec((tq, hd), lambda h, i: (i, h)),
                  pl.BlockSpec((n, hd), lambda h, i: (0, h)),
                  pl.BlockSpec((n, hd), lambda h, i: (0, h)),
                  pl.BlockSpec((1, n // tq, 1, tq), lambda h, i: (h, 0, 0, 0))],
        out_specs=pl.BlockSpec((tq, hd), lambda h, i: (i, h)),
        scratch_shapes=[pltpu.VMEM((tq, 1), F32), pltpu.VMEM((tq, 1), F32), pltpu.VMEM((tq, hd), F32)],
        compiler_params=_params(("parallel", "parallel"), est),
        name="fox_prompt",
    )(q, k, v, f_t)


def _fox_cached_kernel(q_ref, kp_ref, vp_ref, kn_ref, vn_ref, fp_ref, fn_ref, o_ref):
    q = q_ref[0]
    dims = (((1,), (1,)), ((), ()))
    sp = lax.dot_general(q, kp_ref[0].astype(BF16), dims, preferred_element_type=F32) - fp_ref[0, 0]
    sn = lax.dot_general(q, kn_ref[0], dims, preferred_element_type=F32) - fn_ref[0, 0]
    row = lax.broadcasted_iota(jnp.int32, sn.shape, 0)
    col = lax.broadcasted_iota(jnp.int32, sn.shape, 1)
    sn = jnp.where(col <= row, sn, MASK_VALUE)
    m = jnp.maximum(jnp.max(sp, axis=-1, keepdims=True), jnp.max(sn, axis=-1, keepdims=True))
    pp = jnp.exp(sp - m)
    pn = jnp.exp(sn - m)
    denom = jnp.sum(pp, axis=-1, keepdims=True) + jnp.sum(pn, axis=-1, keepdims=True)
    acc = jnp.dot(pp.astype(BF16), vp_ref[0].astype(BF16), preferred_element_type=F32)
    acc = acc + jnp.dot(pn.astype(BF16), vn_ref[0], preferred_element_type=F32)
    o_ref[0] = (acc / denom).astype(o_ref.dtype)


def fox_cached(q, k_past, v_past, k_new, v_new, f_past, f_new, hd):
    b, n, dm = q.shape
    p = k_past.shape[1]
    nh = dm // hd
    fp_t = jnp.transpose(f_past, (0, 2, 1)).reshape(b, nh, 1, p)
    fn_t = jnp.transpose(f_new, (0, 2, 1)).reshape(b, nh, 1, n)
    new = pl.BlockSpec((1, n, hd), lambda bb, h: (bb, 0, h))
    past = pl.BlockSpec((1, p, hd), lambda bb, h: (bb, 0, h))
    est = 2 * (2 * p * hd * 4 + 4 * n * hd * 2) + 4 * 8 * p * 4 + p * hd * 4
    return pl.pallas_call(
        _fox_cached_kernel,
        out_shape=jax.ShapeDtypeStruct((b, n, dm), BF16),
        grid=(b, nh),
        in_specs=[new, past, past, new, new,
                  pl.BlockSpec((1, 1, 1, p), lambda bb, h: (bb, h, 0, 0)),
                  pl.BlockSpec((1, 1, 1, n), lambda bb, h: (bb, h, 0, 0))],
        out_specs=new,
        compiler_params=_params(("parallel", "parallel"), est),
        name="fox_cached",
    )(q, k_past, v_past, k_new, v_new, fp_t, fn_t)


def _trunk(x, pool_prev, mem_k, mem_v, past, w):
    b, n, d = x.shape
    m = b * n
    depth, n_a = w["depth"], w["n_a"]
    c, hd, heads = w["d_pool"], w["hd_fox"], w["mem_heads"]
    assert n >= POOL_STATE
    pos0 = 0 if past is None else past[0].shape[1]
    x2 = x.reshape(m, d)
    h = rms_cast(x2, w["norm_g"][0, 0])
    new_pool = []
    k_new = v_new = logf_new = None
    k_bf = v_bf = f_all = None
    for l in range(depth):
        g = w["norm_g"][l]
        y = mm_kacc(ffn_up(h, w["wg"][l][0], w["wu"][l][0]), w["wd"][l][0])
        x2, (h,) = resid_norm(x2, y, g[1], 0.5, [g[2]])
        if l < n_a:
            w_in, w_out = w["w_in_a"][l], w["w_out_a"][l]
            (u,) = mm(h, w_in[:, :c], [F32])
            u3 = u.reshape(b, n, c)
            main = pool_mixer(u3, pool_prev[l], pos0, w["pool_w"][l], w["pool_scale"][l])
            new_pool.append(u3[:, n - POOL_STATE:, :])
        else:
            j = l - n_a
            w_in, w_out = w["w_in_b"][j], w["w_out_b"][j]
            (q,) = mm(h, w_in[:, :c], [BF16], scale=hd ** -0.5)
            if past is None:
                main = fox_prompt(q, k_bf, v_bf, f_all[0], hd).reshape(b, n, c)
            else:
                p = pos0
                main = fox_cached(q.reshape(b, n, c), past[0], past[1], k_bf.reshape(b, n, c),
                                  v_bf.reshape(b, n, c), f_all[:, :p], f_all[:, p:], hd)
        (qm,) = mm(h, w_in[:, c:], [BF16])
        mem = mem_attend(qm.reshape(b, n, d - c), mem_k[l], mem_v[l], heads)
        mixed = jnp.concatenate([main, mem], axis=-1).reshape(m, d)
        (y,) = mm(mixed, w_out, [F32])
        x2, (h,) = resid_norm(x2, y, g[3], 1.0, [g[4]])
        y = mm_kacc(ffn_up(h, w["wg"][l][1], w["wu"][l][1]), w["wd"][l][1])
        g_next = []
        if l + 1 < depth:
            g_next.append(w["norm_g"][l + 1, 0])
        if l == n_a - 1:
            g_next.append(w["g_kv"])
        x2, hs = resid_norm(x2, y, g[5], 0.5, g_next)
        if l + 1 < depth:
            h = hs[0]
        if l == n_a - 1:
            hk = hs[-1]
            k_new, k_bf = mm(hk, w["w_kv"][:, :c], [F32, BF16])
            v_new, v_bf = mm(hk, w["w_kv"][:, c:], [F32, BF16])
            logf_new = log_forget(hk, w["w_f"], w["b_f"]).reshape(b, n, -1)
            logf_all = logf_new if past is None else jnp.concatenate(
                [past[2].astype(F32), logf_new], axis=1)
            f_all = cumsum_time(logf_all)
    nh = c // hd
    return (x2.reshape(b, n, d), jnp.stack(new_pool), k_new.reshape(b, n, nh, hd),
            v_new.reshape(b, n, nh, hd), logf_new)


def _pad_ff(wmat, axis):
    ff = wmat.shape[axis]
    pad = -ff % FF_ALIGN
    widths = [(0, 0)] * wmat.ndim
    widths[axis] = (0, pad)
    return jnp.pad(wmat, widths).astype(BF16)


def kernel(x_prompt, x_sample, mem_prompt, cache_fox_k, cache_fox_v, cache_fox_logf, cache_mem_k, cache_mem_v, state_pool, norm_g, w_ffn_gate, w_ffn_up, w_ffn_down, w_in_a, pool_w, pool_scale, w_out_a, w_in_b, w_out_b, g_kv, w_kv, w_f, b_f, g_mem, w_mem_kv):
    depth = norm_g.shape[0]
    n_a, db, _, c = state_pool.shape
    _, p, nh, hd = cache_fox_k.shape
    _, _, n_mem, heads, mem_hd = cache_mem_k.shape
    dq = heads * mem_hd
    bp, _, d = x_prompt.shape
    w = dict(
        depth=depth, n_a=n_a, d_pool=c, hd_fox=hd, mem_heads=heads,
        norm_g=norm_g, g_kv=g_kv, w_f=w_f, b_f=b_f, pool_scale=pool_scale,
        wg=[[_pad_ff(w_ffn_gate[l, s], 1) for s in range(2)] for l in range(depth)],
        wu=[[_pad_ff(w_ffn_up[l, s], 1) for s in range(2)] for l in range(depth)],
        wd=[[_pad_ff(w_ffn_down[l, s], 0) for s in range(2)] for l in range(depth)],
        w_in_a=w_in_a.astype(BF16), w_out_a=w_out_a.astype(BF16),
        w_in_b=w_in_b.astype(BF16), w_out_b=w_out_b.astype(BF16),
        pool_w=pool_w.astype(BF16), w_kv=w_kv.astype(BF16),
    )
    mem2 = mem_prompt.reshape(bp * n_mem, d)
    mem_kv = [mm(rms_cast(mem2, g_mem[l]), w_mem_kv[l].astype(BF16), [F32])[0] for l in range(depth)]
    mem_k_p = jnp.stack([kv[:, :dq].reshape(bp, n_mem, dq) for kv in mem_kv])
    mem_v_p = jnp.stack([kv[:, dq:].reshape(bp, n_mem, dq) for kv in mem_kv])

    zero_prev = jnp.zeros((n_a, bp, HALO, c), F32)
    y_p, pool_p, k_p, v_p, logf_p = _trunk(x_prompt, zero_prev, mem_k_p, mem_v_p, None, w)

    prev_s = jnp.pad(state_pool.astype(F32), ((0, 0), (0, 0), (1, 0), (0, 0)))
    past = (cache_fox_k.reshape(db, p, nh * hd), cache_fox_v.reshape(db, p, nh * hd), cache_fox_logf)
    y_s, pool_s, k_s, v_s, logf_s = _trunk(
        x_sample, prev_s, cache_mem_k.reshape(depth, db, n_mem, dq),
        cache_mem_v.reshape(depth, db, n_mem, dq), past, w)

    return (y_p, y_s, k_p, v_p, logf_p,
            mem_k_p.reshape(depth, bp, n_mem, heads, mem_hd),
            mem_v_p.reshape(depth, bp, n_mem, heads, mem_hd),
            pool_p, k_s, v_s, logf_s, pool_s)
```

```python
import functools

import jax
import jax.numpy as jnp
from jax import lax
from jax.experimental import pallas as pl
from jax.experimental.pallas import tpu as pltpu

EPS = 1e-6
POOL_WINDOWS = (2, 4, 8, 16)
POOL_STATE = max(POOL_WINDOWS) - 1
HALO = POOL_STATE + 1
LANE = 128
FF_ALIGN = 1024
MASK_VALUE = -1e30
VMEM_CAP = 60 * 1024 * 1024
F32 = jnp.float32
BF16 = jnp.bfloat16


def _tile(n, pref):
    if n <= pref:
        return n
    t = pref
    while n % t:
        t //= 2
    return t


def _params(sem, est_bytes):
    limit = int(min(VMEM_CAP, max(32 * 1024 * 1024, est_bytes * 5 // 4 + (4 << 20))))
    return pltpu.CompilerParams(dimension_semantics=sem, vmem_limit_bytes=limit)


def _rms_normalize(xf):
    return xf * lax.rsqrt(jnp.mean(xf * xf, axis=-1, keepdims=True) + EPS)


def _rms_cast_kernel(x_ref, g_ref, o_ref):
    o_ref[...] = (_rms_normalize(x_ref[...]) * g_ref[...]).astype(o_ref.dtype)


def rms_cast(x, g):
    m, d = x.shape
    tm = _tile(m, 512)
    return pl.pallas_call(
        _rms_cast_kernel,
        out_shape=jax.ShapeDtypeStruct((m, d), BF16),
        grid=(m // tm,),
        in_specs=[pl.BlockSpec((tm, d), lambda i: (i, 0)),
                  pl.BlockSpec((1, d), lambda i: (0, 0))],
        out_specs=pl.BlockSpec((tm, d), lambda i: (i, 0)),
        compiler_params=_params(("parallel",), 2 * tm * d * 6),
        name="rms_cast",
    )(x, g.reshape(1, d).astype(F32))


def _mm_kernel(h_ref, w_ref, *o_refs, scale):
    acc = jnp.dot(h_ref[...], w_ref[...], preferred_element_type=F32)
    if scale is not None:
        acc = acc * scale
    for o_ref in o_refs:
        o_ref[...] = acc.astype(o_ref.dtype)


def mm(h, w, out_dtypes, scale=None, tm_pref=1024, tn_pref=1024):
    m, k = h.shape
    n = w.shape[1]
    tm, tn = _tile(m, tm_pref), _tile(n, tn_pref)
    est = 2 * (tm * k * 2 + k * tn * 2) + tm * tn * 4 + sum(
        2 * tm * tn * jnp.dtype(t).itemsize for t in out_dtypes)
    outs = pl.pallas_call(
        functools.partial(_mm_kernel, scale=scale),
        out_shape=[jax.ShapeDtypeStruct((m, n), t) for t in out_dtypes],
        grid=(m // tm, n // tn),
        in_specs=[pl.BlockSpec((tm, k), lambda i, j: (i, 0)),
                  pl.BlockSpec((k, tn), lambda i, j: (0, j))],
        out_specs=[pl.BlockSpec((tm, tn), lambda i, j: (i, j)) for _ in out_dtypes],
        compiler_params=_params(("parallel", "parallel"), est),
        name="mm",
    )(h, w)
    return outs


def _ffn_up_kernel(h_ref, wg_ref, wu_ref, o_ref):
    h = h_ref[...]
    g = jnp.dot(h, wg_ref[...], preferred_element_type=F32)
    u = jnp.dot(h, wu_ref[...], preferred_element_type=F32)
    o_ref[...] = (g * jax.nn.sigmoid(g) * u).astype(o_ref.dtype)


def ffn_up(h, wg, wu):
    m, k = h.shape
    n = wg.shape[1]
    tm, tn = _tile(m, 1024), _tile(n, 512)
    est = 2 * (tm * k * 2 + 2 * k * tn * 2 + tm * tn * 2) + 3 * tm * tn * 4
    return pl.pallas_call(
        _ffn_up_kernel,
        out_shape=jax.ShapeDtypeStruct((m, n), BF16),
        grid=(m // tm, n // tn),
        in_specs=[pl.BlockSpec((tm, k), lambda i, j: (i, 0)),
                  pl.BlockSpec((k, tn), lambda i, j: (0, j)),
                  pl.BlockSpec((k, tn), lambda i, j: (0, j))],
        out_specs=pl.BlockSpec((tm, tn), lambda i, j: (i, j)),
        compiler_params=_params(("parallel", "parallel"), est),
        name="ffn_up",
    )(h, wg, wu)


EPILOGUE_ROWS = 128


def _mm_resid_kernel(*refs, kblocks, coef, n_next, tm):
    n_lhs = len(kblocks)
    a_refs = refs[:n_lhs]
    w_ref, x_ref, gp_ref = refs[n_lhs:n_lhs + 3]
    gn_refs = refs[n_lhs + 3:n_lhs + 3 + n_next]
    xo_ref = refs[n_lhs + 3 + n_next]
    h_refs = refs[n_lhs + 4 + n_next:]
    k = pl.program_id(1)
    nk = sum(kblocks)

    @pl.when(k == 0)
    def _():
        xo_ref[...] = jnp.zeros_like(xo_ref)

    if n_lhs == 1:
        xo_ref[...] += jnp.dot(a_refs[0][...], w_ref[...], preferred_element_type=F32)
    else:
        start = 0
        for a_ref, nb in zip(a_refs, kblocks):
            @pl.when((k >= start) & (k < start + nb))
            def _(a_ref=a_ref):
                xo_ref[...] += jnp.dot(a_ref[...], w_ref[...], preferred_element_type=F32)
            start += nb

    @pl.when(k == nk - 1)
    def _():
        rows = min(EPILOGUE_ROWS, tm)
        for r0 in range(0, tm, rows):
            sl = slice(r0, r0 + rows)
            xn = x_ref[sl, :] + coef * (_rms_normalize(xo_ref[sl, :]) * gp_ref[...])
            xo_ref[sl, :] = xn
            if n_next:
                r = _rms_normalize(xn)
                for gn_ref, h_ref in zip(gn_refs, h_refs):
                    h_ref[sl, :] = (r * gn_ref[...]).astype(h_ref.dtype)


def mm_resid(lhs, w, x, g_post, coef, g_next):
    m, d = x.shape
    widths = [a.shape[1] for a in lhs]
    assert sum(widths) == w.shape[0] and w.shape[1] == d
    tk = 1024
    while any(kw % tk for kw in widths):
        tk //= 2
    tm = _tile(m, 512)
    kblocks = tuple(kw // tk for kw in widths)
    starts = [sum(kblocks[:j]) for j in range(len(lhs))]
    n_next = len(g_next)
    single = pl.Buffered(1)
    row_in = pl.BlockSpec((tm, d), lambda i, k: (i, 0), pipeline_mode=single)
    row_out = pl.BlockSpec((tm, d), lambda i, k: (i, 0), pipeline_mode=single)
    gain = pl.BlockSpec((1, d), lambda i, k: (0, 0))

    def lhs_spec(start, nb):
        return pl.BlockSpec((tm, tk), lambda i, k: (i, jnp.clip(k - start, 0, nb - 1)))

    est = len(lhs) * 2 * tm * tk * 2 + 2 * tk * d * 2 + 2 * tm * d * 4 + n_next * tm * d * 2 \
        + tm * d * 4 + 6 * EPILOGUE_ROWS * d * 4
    outs = pl.pallas_call(
        functools.partial(_mm_resid_kernel, kblocks=kblocks, coef=coef, n_next=n_next, tm=tm),
        out_shape=[jax.ShapeDtypeStruct((m, d), F32)]
        + [jax.ShapeDtypeStruct((m, d), BF16) for _ in range(n_next)],
        grid=(m // tm, sum(kblocks)),
        in_specs=[lhs_spec(s, nb) for s, nb in zip(starts, kblocks)]
        + [pl.BlockSpec((tk, d), lambda i, k: (k, 0)), row_in, gain] + [gain] * n_next,
        out_specs=[row_out] * (1 + n_next),
        compiler_params=_params(("parallel", "arbitrary"), est),
        name="mm_resid",
    )(*lhs, w, x, g_post.reshape(1, d).astype(F32), *[g.reshape(1, d).astype(F32) for g in g_next])
    return outs[0], outs[1:]


def _pool_kernel(u_ref, halo_ref, prev_ref, w_ref, sc_ref, o_ref, ext_ref, *, tm, pos0, group):
    i = pl.program_id(1)

    @pl.when(i == 0)
    def _():
        ext_ref[0:HALO, :] = prev_ref[0]

    @pl.when(i > 0)
    def _():
        ext_ref[0:HALO, :] = halo_ref[0]

    ext_ref[HALO:HALO + tm, :] = u_ref[0]
    pos = pos0 + i * tm + lax.broadcasted_iota(jnp.int32, (tm, 1), 0)
    for gi, win in enumerate(POOL_WINDOWS):
        sl = slice(gi * group, (gi + 1) * group)
        ug = ext_ref[HALO:HALO + tm, sl]
        wsum = ug
        for j in range(1, win):
            wsum = wsum + ext_ref[HALO - j:HALO - j + tm, sl]
        cnt = jnp.minimum(pos + 1, win).astype(F32)
        diff = wsum / cnt - ug
        y = jnp.dot(diff.astype(BF16), w_ref[gi], preferred_element_type=F32)
        o_ref[0, :, sl] = (y * sc_ref[:, sl]).astype(o_ref.dtype)


def pool_mixer(u, prev, pos0, pool_w, pool_scale):
    b, n, c = u.shape
    group = c // len(POOL_WINDOWS)
    tm = _tile(n, 256)
    steps = tm // HALO
    est = 2 * (tm * c * 4 + 2 * HALO * c * 4 + pool_w.size * 2 + tm * c * 2) + (tm + HALO) * c * 4 \
        + 4 * tm * group * 4
    return pl.pallas_call(
        functools.partial(_pool_kernel, tm=tm, pos0=pos0, group=group),
        out_shape=jax.ShapeDtypeStruct((b, n, c), BF16),
        grid=(b, n // tm),
        in_specs=[pl.BlockSpec((1, tm, c), lambda bb, i: (bb, i, 0)),
                  pl.BlockSpec((1, HALO, c), lambda bb, i: (bb, jnp.maximum(i * steps - 1, 0), 0)),
                  pl.BlockSpec((1, HALO, c), lambda bb, i: (bb, 0, 0)),
                  pl.BlockSpec(pool_w.shape, lambda bb, i: (0, 0, 0)),
                  pl.BlockSpec((1, c), lambda bb, i: (0, 0))],
        out_specs=pl.BlockSpec((1, tm, c), lambda bb, i: (bb, i, 0)),
        scratch_shapes=[pltpu.VMEM((tm + HALO, c), F32)],
        compiler_params=_params(("parallel", "parallel"), est),
        name="pool_mixer",
    )(u, u, prev, pool_w, pool_scale.reshape(1, c).astype(F32))


def _mem_attn_kernel(q_ref, k_ref, v_ref, o_ref, *, heads, hd):
    scale = hd ** -0.5
    for hh in range(heads):
        sl = slice(hh * hd, (hh + 1) * hd)
        q = q_ref[0, :, sl]
        k = k_ref[0, :, sl].astype(BF16)
        v = v_ref[0, :, sl].astype(BF16)
        s = lax.dot_general(q, k, (((1,), (1,)), ((), ())), preferred_element_type=F32) * scale
        p = jnp.exp(s - jnp.max(s, axis=-1, keepdims=True))
        p = p / jnp.sum(p, axis=-1, keepdims=True)
        o_ref[0, :, sl] = jnp.dot(p.astype(BF16), v, preferred_element_type=F32).astype(o_ref.dtype)


def mem_attend(qm, mk, mv, heads):
    b, n, dq = qm.shape
    nm = mk.shape[1]
    tm = _tile(n, 512)
    est = 2 * (2 * tm * dq * 2 + 2 * nm * dq * 4) + 4 * tm * nm * 4
    return pl.pallas_call(
        functools.partial(_mem_attn_kernel, heads=heads, hd=dq // heads),
        out_shape=jax.ShapeDtypeStruct((b, n, dq), BF16),
        grid=(b, n // tm),
        in_specs=[pl.BlockSpec((1, tm, dq), lambda bb, i: (bb, i, 0)),
                  pl.BlockSpec((1, nm, dq), lambda bb, i: (bb, 0, 0)),
                  pl.BlockSpec((1, nm, dq), lambda bb, i: (bb, 0, 0))],
        out_specs=pl.BlockSpec((1, tm, dq), lambda bb, i: (bb, i, 0)),
        compiler_params=_params(("parallel", "parallel"), est),
        name="mem_attend",
    )(qm, mk, mv)


def _logf_kernel(h_ref, w_ref, b_ref, o_ref):
    z = jnp.dot(h_ref[...], w_ref[...], preferred_element_type=F32) + b_ref[...]
    o_ref[...] = jnp.minimum(z, 0.0) - jnp.log1p(jnp.exp(-jnp.abs(z)))


def log_forget(hk, w_f, b_f):
    m, k = hk.shape
    nh = w_f.shape[1]
    wp = jnp.pad(w_f, ((0, 0), (0, LANE - nh))).astype(BF16)
    bp = jnp.pad(b_f, (0, LANE - nh)).reshape(1, LANE).astype(F32)
    tm = _tile(m, 1024)
    out = pl.pallas_call(
        _logf_kernel,
        out_shape=jax.ShapeDtypeStruct((m, LANE), F32),
        grid=(m // tm,),
        in_specs=[pl.BlockSpec((tm, k), lambda i: (i, 0)),
                  pl.BlockSpec((k, LANE), lambda i: (0, 0)),
                  pl.BlockSpec((1, LANE), lambda i: (0, 0))],
        out_specs=pl.BlockSpec((tm, LANE), lambda i: (i, 0)),
        compiler_params=_params(("parallel",), 2 * (tm * k * 2 + k * LANE * 2 + tm * LANE * 4)),
        name="log_forget",
    )(hk, wp, bp)
    return out[:, :nh]


def _cumsum_kernel(x_ref, o_ref, carry_ref, *, tb):
    @pl.when(pl.program_id(1) == 0)
    def _():
        carry_ref[...] = jnp.zeros_like(carry_ref)

    x = x_ref[0]
    row = lax.broadcasted_iota(jnp.int32, x.shape, 0)
    d = 1
    while d < tb:
        x = x + jnp.where(row >= d, pltpu.roll(x, d, 0), 0.0)
        d *= 2
    x = x + carry_ref[0:1, :]
    o_ref[0] = x
    carry_ref[...] = jnp.broadcast_to(x[tb - 1:tb, :], carry_ref.shape)


def cumsum_time(logf):
    b, t, nh = logf.shape
    tb = 256
    tp = -(-t // tb) * tb
    x = jnp.pad(logf, ((0, 0), (0, tp - t), (0, LANE - nh)))
    out = pl.pallas_call(
        functools.partial(_cumsum_kernel, tb=tb),
        out_shape=jax.ShapeDtypeStruct((b, tp, LANE), F32),
        grid=(b, tp // tb),
        in_specs=[pl.BlockSpec((1, tb, LANE), lambda bb, i: (bb, i, 0))],
        out_specs=pl.BlockSpec((1, tb, LANE), lambda bb, i: (bb, i, 0)),
        scratch_shapes=[pltpu.VMEM((8, LANE), F32)],
        compiler_params=_params(("parallel", "arbitrary"), 8 * tb * LANE * 4),
        name="cumsum_time",
    )(x)
    return out[:, :t, :nh]


LOG2E = 1.4426950408889634
_NT = (((1,), (1,)), ((), ()))


def _lanes(x, reps):
    return x if reps == 1 else jnp.concatenate([x] * reps, axis=1)


def _fox_prompt_kernel(q_ref, k_ref, v_ref, f_ref, o_ref, s0_ref, s1_ref, m_ref, acc_ref,
                       *, tq, hd, nhb):
    qi = pl.program_id(1)
    m_ref[...] = jnp.full_like(m_ref, MASK_VALUE)
    acc_ref[...] = jnp.zeros_like(acc_ref)
    ones = jnp.ones((tq, hd), BF16)

    def logits(kb, s_ref):
        off = pl.multiple_of(kb * tq, tq)
        for hh in range(nhb):
            sl = slice(hh * hd, (hh + 1) * hd)
            s = lax.dot_general(q_ref[:, sl], k_ref[pl.ds(off, tq), sl], _NT,
                                preferred_element_type=F32)
            s_ref[hh] = s - f_ref[hh, kb] * LOG2E

    def attend(kb, s_ref, masked):
        off = pl.multiple_of(kb * tq, tq)
        for hh in range(nhb):
            sl = slice(hh * hd, (hh + 1) * hd)
            s = s_ref[hh]
            if masked:
                row = lax.broadcasted_iota(jnp.int32, s.shape, 0)
                col = lax.broadcasted_iota(jnp.int32, s.shape, 1)
                s = jnp.where(col <= row, s, MASK_VALUE)
            m_prev = m_ref[hh]
            m_new = jnp.maximum(m_prev, jnp.max(s, axis=-1, keepdims=True))
            alpha = jnp.exp2(m_prev - m_new)
            p = jnp.exp2(s - _lanes(m_new, tq // LANE))
            v = v_ref[pl.ds(off, tq), sl]
            pv = jnp.dot(p.astype(BF16), jnp.concatenate([v, ones], axis=1),
                         preferred_element_type=F32)
            acc_ref[hh] = _lanes(alpha, 2 * hd // LANE) * acc_ref[hh] + pv
            m_ref[hh] = m_new

    def body(jj, carry):
        kb = 2 * jj
        logits(kb + 1, s1_ref)
        attend(kb, s0_ref, False)
        logits(kb + 2, s0_ref)
        attend(kb + 1, s1_ref, False)
        return carry

    logits(0, s0_ref)
    lax.fori_loop(0, qi // 2, body, 0)

    @pl.when(qi % 2 == 0)
    def _():
        attend(qi, s0_ref, True)

    @pl.when(qi % 2 == 1)
    def _():
        logits(qi, s1_ref)
        attend(qi - 1, s0_ref, False)
        attend(qi, s1_ref, True)

    for hh in range(nhb):
        acc = acc_ref[hh]
        o_ref[:, hh * hd:(hh + 1) * hd] = (acc[:, :hd] / acc[:, hd:]).astype(o_ref.dtype)


def fox_prompt(q, k, v, f_cum, hd):
    n, dm = q.shape
    nh = dm // hd
    nhb = 2 if nh % 2 == 0 else 1
    tq = _tile(n, 512)
    assert hd == LANE and tq % LANE == 0
    f_t = f_cum.T.reshape(nh, n // tq, 1, tq)
    wb = nhb * hd
    est = 2 * (2 * tq * wb * 2 + 2 * n * wb * 2 + nhb * (n // tq) * 8 * tq * 4) \
        + nhb * (tq * LANE * 4 + tq * 2 * hd * 4 + 5 * tq * tq * 4)
    return pl.pallas_call(
        functools.partial(_fox_prompt_kernel, tq=tq, hd=hd, nhb=nhb),
        out_shape=jax.ShapeDtypeStruct((n, dm), BF16),
        grid=(nh // nhb, n // tq),
        in_specs=[pl.BlockSpec((tq, wb), lambda h, i: (i, h)),
                  pl.BlockSpec((n, wb), lambda h, i: (0, h)),
                  pl.BlockSpec((n, wb), lambda h, i: (0, h)),
                  pl.BlockSpec((nhb, n // tq, 1, tq), lambda h, i: (h, 0, 0, 0))],
        out_specs=pl.BlockSpec((tq, wb), lambda h, i: (i, h)),
        scratch_shapes=[pltpu.VMEM((nhb, tq, tq), F32), pltpu.VMEM((nhb, tq, tq), F32),
                        pltpu.VMEM((nhb, tq, LANE), F32), pltpu.VMEM((nhb, tq, 2 * hd), F32)],
        compiler_params=_params(("parallel", "parallel"), est),
        name="fox_prompt",
    )(q, k, v, f_t)


def _fox_cached_kernel(q_ref, kp_ref, vp_ref, kn_ref, vn_ref, fp_ref, fn_ref, o_ref):
    q = q_ref[0]
    sp = lax.dot_general(q, kp_ref[0].astype(BF16), _NT, preferred_element_type=F32) \
        - fp_ref[0, 0] * LOG2E
    sn = lax.dot_general(q, kn_ref[0], _NT, preferred_element_type=F32) - fn_ref[0, 0] * LOG2E
    row = lax.broadcasted_iota(jnp.int32, sn.shape, 0)
    col = lax.broadcasted_iota(jnp.int32, sn.shape, 1)
    sn = jnp.where(col <= row, sn, MASK_VALUE)
    m = jnp.maximum(jnp.max(sp, axis=-1, keepdims=True), jnp.max(sn, axis=-1, keepdims=True))
    pp = jnp.exp2(sp - m)
    pn = jnp.exp2(sn - m)
    denom = jnp.sum(pp, axis=-1, keepdims=True) + jnp.sum(pn, axis=-1, keepdims=True)
    acc = jnp.dot(pp.astype(BF16), vp_ref[0].astype(BF16), preferred_element_type=F32)
    acc = acc + jnp.dot(pn.astype(BF16), vn_ref[0], preferred_element_type=F32)
    o_ref[0] = (acc / denom).astype(o_ref.dtype)


def fox_cached(q, k_past, v_past, k_new, v_new, f_past, f_new, hd):
    b, n, dm = q.shape
    p = k_past.shape[1]
    nh = dm // hd
    fp_t = jnp.transpose(f_past, (0, 2, 1)).reshape(b, nh, 1, p)
    fn_t = jnp.transpose(f_new, (0, 2, 1)).reshape(b, nh, 1, n)
    new = pl.BlockSpec((1, n, hd), lambda bb, h: (bb, 0, h))
    past = pl.BlockSpec((1, p, hd), lambda bb, h: (bb, 0, h))
    est = 2 * (2 * p * hd * 4 + 4 * n * hd * 2) + 4 * 8 * p * 4 + p * hd * 4
    return pl.pallas_call(
        _fox_cached_kernel,
        out_shape=jax.ShapeDtypeStruct((b, n, dm), BF16),
        grid=(b, nh),
        in_specs=[new, past, past, new, new,
                  pl.BlockSpec((1, 1, 1, p), lambda bb, h: (bb, h, 0, 0)),
                  pl.BlockSpec((1, 1, 1, n), lambda bb, h: (bb, h, 0, 0))],
        out_specs=new,
        compiler_params=_params(("parallel", "parallel"), est),
        name="fox_cached",
    )(q, k_past, v_past, k_new, v_new, fp_t, fn_t)


def _trunk(x, pool_prev, mem_k, mem_v, past, w):
    b, n, d = x.shape
    m = b * n
    depth, n_a = w["depth"], w["n_a"]
    c, hd, heads = w["d_pool"], w["hd_fox"], w["mem_heads"]
    assert n >= POOL_STATE
    pos0 = 0 if past is None else past[0].shape[1]
    x2 = x.reshape(m, d)
    h = rms_cast(x2, w["norm_g"][0, 0])
    new_pool = []
    k_new = v_new = logf_new = None
    k_bf = v_bf = f_all = None
    for l in range(depth):
        g = w["norm_g"][l]
        a = ffn_up(h, w["wg"][l][0], w["wu"][l][0])
        x2, (h,) = mm_resid([a], w["wd"][l][0], x2, g[1], 0.5, [g[2]])
        if l < n_a:
            w_in, w_out = w["w_in_a"][l], w["w_out_a"][l]
            (u,) = mm(h, w_in[:, :c], [F32])
            u3 = u.reshape(b, n, c)
            main = pool_mixer(u3, pool_prev[l], pos0, w["pool_w"][l], w["pool_scale"][l])
            new_pool.append(u3[:, n - POOL_STATE:, :])
        else:
            j = l - n_a
            w_in, w_out = w["w_in_b"][j], w["w_out_b"][j]
            (q,) = mm(h, w_in[:, :c], [BF16], scale=LOG2E * hd ** -0.5)
            if past is None:
                main = fox_prompt(q, k_bf, v_bf, f_all[0], hd).reshape(b, n, c)
            else:
                p = pos0
                main = fox_cached(q.reshape(b, n, c), past[0], past[1], k_bf.reshape(b, n, c),
                                  v_bf.reshape(b, n, c), f_all[:, :p], f_all[:, p:], hd)
        (qm,) = mm(h, w_in[:, c:], [BF16])
        mem = mem_attend(qm.reshape(b, n, d - c), mem_k[l], mem_v[l], heads)
        x2, (h,) = mm_resid([main.reshape(m, c), mem.reshape(m, d - c)], w_out, x2, g[3], 1.0, [g[4]])
        a = ffn_up(h, w["wg"][l][1], w["wu"][l][1])
        g_next = []
        if l + 1 < depth:
            g_next.append(w["norm_g"][l + 1, 0])
        if l == n_a - 1:
            g_next.append(w["g_kv"])
        x2, hs = mm_resid([a], w["wd"][l][1], x2, g[5], 0.5, g_next)
        if l + 1 < depth:
            h = hs[0]
        if l == n_a - 1:
            hk = hs[-1]
            k_new, k_bf = mm(hk, w["w_kv"][:, :c], [F32, BF16])
            v_new, v_bf = mm(hk, w["w_kv"][:, c:], [F32, BF16])
            logf_new = log_forget(hk, w["w_f"], w["b_f"]).reshape(b, n, -1)
            logf_all = logf_new if past is None else jnp.concatenate(
                [past[2].astype(F32), logf_new], axis=1)
            f_all = cumsum_time(logf_all)
    nh = c // hd
    return (x2.reshape(b, n, d), jnp.stack(new_pool), k_new.reshape(b, n, nh, hd),
            v_new.reshape(b, n, nh, hd), logf_new)


def _pad_ff(wmat, axis):
    ff = wmat.shape[axis]
    pad = -ff % FF_ALIGN
    widths = [(0, 0)] * wmat.ndim
    widths[axis] = (0, pad)
    return jnp.pad(wmat, widths).astype(BF16)


def kernel(x_prompt, x_sample, mem_prompt, cache_fox_k, cache_fox_v, cache_fox_logf, cache_mem_k, cache_mem_v, state_pool, norm_g, w_ffn_gate, w_ffn_up, w_ffn_down, w_in_a, pool_w, pool_scale, w_out_a, w_in_b, w_out_b, g_kv, w_kv, w_f, b_f, g_mem, w_mem_kv):
    depth = norm_g.shape[0]
    n_a, db, _, c = state_pool.shape
    _, p, nh, hd = cache_fox_k.shape
    _, _, n_mem, heads, mem_hd = cache_mem_k.shape
    dq = heads * mem_hd
    bp, _, d = x_prompt.shape
    w = dict(
        depth=depth, n_a=n_a, d_pool=c, hd_fox=hd, mem_heads=heads,
        norm_g=norm_g, g_kv=g_kv, w_f=w_f, b_f=b_f, pool_scale=pool_scale,
        wg=[[_pad_ff(w_ffn_gate[l, s], 1) for s in range(2)] for l in range(depth)],
        wu=[[_pad_ff(w_ffn_up[l, s], 1) for s in range(2)] for l in range(depth)],
        wd=[[_pad_ff(w_ffn_down[l, s], 0) for s in range(2)] for l in range(depth)],
        w_in_a=w_in_a.astype(BF16), w_out_a=w_out_a.astype(BF16),
        w_in_b=w_in_b.astype(BF16), w_out_b=w_out_b.astype(BF16),
        pool_w=pool_w.astype(BF16), w_kv=w_kv.astype(BF16),
    )
    mem2 = mem_prompt.reshape(bp * n_mem, d)
    mem_kv = [mm(rms_cast(mem2, g_mem[l]), w_mem_kv[l].astype(BF16), [F32])[0] for l in range(depth)]
    mem_k_p = jnp.stack([kv[:, :dq].reshape(bp, n_mem, dq) for kv in mem_kv])
    mem_v_p = jnp.stack([kv[:, dq:].reshape(bp, n_mem, dq) for kv in mem_kv])

    zero_prev = jnp.zeros((n_a, bp, HALO, c), F32)
    y_p, pool_p, k_p, v_p, logf_p = _trunk(x_prompt, zero_prev, mem_k_p, mem_v_p, None, w)

    prev_s = jnp.pad(state_pool.astype(F32), ((0, 0), (0, 0), (1, 0), (0, 0)))
    past = (cache_fox_k.reshape(db, p, nh * hd), cache_fox_v.reshape(db, p, nh * hd), cache_fox_logf)
    y_s, pool_s, k_s, v_s, logf_s = _trunk(
        x_sample, prev_s, cache_mem_k.reshape(depth, db, n_mem, dq),
        cache_mem_v.reshape(depth, db, n_mem, dq), past, w)

    return (y_p, y_s, k_p, v_p, logf_p,
            mem_k_p.reshape(depth, bp, n_mem, heads, mem_hd),
            mem_v_p.reshape(depth, bp, n_mem, heads, mem_hd),
            pool_p, k_s, v_s, logf_s, pool_s)
```

```python
import functools

import jax
import jax.numpy as jnp
from jax import lax
from jax.experimental import pallas as pl
from jax.experimental.pallas import tpu as pltpu

EPS = 1e-6
POOL_WINDOWS = (2, 4, 8, 16)
POOL_STATE = max(POOL_WINDOWS) - 1
HALO = POOL_STATE + 1
LANE = 128
FF_ALIGN = 1024
MASK_VALUE = -1e30
VMEM_CAP = 60 * 1024 * 1024
F32 = jnp.float32
BF16 = jnp.bfloat16


def _tile(n, pref):
    if n <= pref:
        return n
    t = pref
    while n % t:
        t //= 2
    return t


def _params(sem, est_bytes):
    limit = int(min(VMEM_CAP, max(32 * 1024 * 1024, est_bytes * 5 // 4 + (4 << 20))))
    return pltpu.CompilerParams(dimension_semantics=sem, vmem_limit_bytes=limit)


def _rms_normalize(xf):
    return xf * lax.rsqrt(jnp.mean(xf * xf, axis=-1, keepdims=True) + EPS)


def _rms_cast_kernel(x_ref, g_ref, o_ref):
    o_ref[...] = (_rms_normalize(x_ref[...]) * g_ref[...]).astype(o_ref.dtype)


def rms_cast(x, g):
    m, d = x.shape
    tm = _tile(m, 512)
    return pl.pallas_call(
        _rms_cast_kernel,
        out_shape=jax.ShapeDtypeStruct((m, d), BF16),
        grid=(m // tm,),
        in_specs=[pl.BlockSpec((tm, d), lambda i: (i, 0)),
                  pl.BlockSpec((1, d), lambda i: (0, 0))],
        out_specs=pl.BlockSpec((tm, d), lambda i: (i, 0)),
        compiler_params=_params(("parallel",), 2 * tm * d * 6),
        name="rms_cast",
    )(x, g.reshape(1, d).astype(F32))


def _mm_kernel(h_ref, w_ref, *o_refs, scale):
    acc = jnp.dot(h_ref[...], w_ref[...], preferred_element_type=F32)
    if scale is not None:
        acc = acc * scale
    for o_ref in o_refs:
        o_ref[...] = acc.astype(o_ref.dtype)


def mm(h, w, out_dtypes, scale=None, tm_pref=1024, tn_pref=1024):
    m, k = h.shape
    n = w.shape[1]
    tm, tn = _tile(m, tm_pref), _tile(n, tn_pref)
    est = 2 * (tm * k * 2 + k * tn * 2) + tm * tn * 4 + sum(
        2 * tm * tn * jnp.dtype(t).itemsize for t in out_dtypes)
    outs = pl.pallas_call(
        functools.partial(_mm_kernel, scale=scale),
        out_shape=[jax.ShapeDtypeStruct((m, n), t) for t in out_dtypes],
        grid=(m // tm, n // tn),
        in_specs=[pl.BlockSpec((tm, k), lambda i, j: (i, 0)),
                  pl.BlockSpec((k, tn), lambda i, j: (0, j))],
        out_specs=[pl.BlockSpec((tm, tn), lambda i, j: (i, j)) for _ in out_dtypes],
        compiler_params=_params(("parallel", "parallel"), est),
        name="mm",
    )(h, w)
    return outs


def _ffn_up_kernel(h_ref, wg_ref, wu_ref, o_ref):
    h = h_ref[...]
    g = jnp.dot(h, wg_ref[...], preferred_element_type=F32)
    u = jnp.dot(h, wu_ref[...], preferred_element_type=F32)
    o_ref[...] = (g * jax.nn.sigmoid(g) * u).astype(o_ref.dtype)


def ffn_up(h, wg, wu):
    m, k = h.shape
    n = wg.shape[1]
    tm, tn = _tile(m, 1024), _tile(n, 512)
    est = 2 * (tm * k * 2 + 2 * k * tn * 2 + tm * tn * 2) + 3 * tm * tn * 4
    return pl.pallas_call(
        _ffn_up_kernel,
        out_shape=jax.ShapeDtypeStruct((m, n), BF16),
        grid=(m // tm, n // tn),
        in_specs=[pl.BlockSpec((tm, k), lambda i, j: (i, 0)),
                  pl.BlockSpec((k, tn), lambda i, j: (0, j)),
                  pl.BlockSpec((k, tn), lambda i, j: (0, j))],
        out_specs=pl.BlockSpec((tm, tn), lambda i, j: (i, j)),
        compiler_params=_params(("parallel", "parallel"), est),
        name="ffn_up",
    )(h, wg, wu)


EPILOGUE_ROWS = 128


def _mm_resid_kernel(*refs, kblocks, coef, n_next, tm, xchunks):
    n_lhs = len(kblocks)
    a_refs = refs[:n_lhs]
    w_ref, xc_ref, gp_ref = refs[n_lhs:n_lhs + 3]
    gn_refs = refs[n_lhs + 3:n_lhs + 3 + n_next]
    xo_ref = refs[n_lhs + 3 + n_next]
    h_refs = refs[n_lhs + 4 + n_next:-1]
    x_ref = refs[-1]
    k = pl.program_id(1)
    nk = sum(kblocks)
    cw = xc_ref.shape[1]

    for c in range(xchunks):
        @pl.when(k == c)
        def _(c=c):
            x_ref[:, c * cw:(c + 1) * cw] = xc_ref[...]

    @pl.when(k == 0)
    def _():
        xo_ref[...] = jnp.zeros_like(xo_ref)

    if n_lhs == 1:
        xo_ref[...] += jnp.dot(a_refs[0][...], w_ref[...], preferred_element_type=F32)
    else:
        start = 0
        for a_ref, nb in zip(a_refs, kblocks):
            @pl.when((k >= start) & (k < start + nb))
            def _(a_ref=a_ref):
                xo_ref[...] += jnp.dot(a_ref[...], w_ref[...], preferred_element_type=F32)
            start += nb

    @pl.when(k == nk - 1)
    def _():
        rows = min(EPILOGUE_ROWS, tm)
        for r0 in range(0, tm, rows):
            sl = slice(r0, r0 + rows)
            xn = x_ref[sl, :] + coef * (_rms_normalize(xo_ref[sl, :]) * gp_ref[...])
            xo_ref[sl, :] = xn
            if n_next:
                r = _rms_normalize(xn)
                for gn_ref, h_ref in zip(gn_refs, h_refs):
                    h_ref[sl, :] = (r * gn_ref[...]).astype(h_ref.dtype)


def mm_resid(lhs, w, x, g_post, coef, g_next):
    m, d = x.shape
    widths = [a.shape[1] for a in lhs]
    assert sum(widths) == w.shape[0] and w.shape[1] == d
    tk = 1024
    while any(kw % tk for kw in widths):
        tk //= 2
    tm = _tile(m, 512)
    kblocks = tuple(kw // tk for kw in widths)
    starts = [sum(kblocks[:j]) for j in range(len(lhs))]
    n_next = len(g_next)
    nk = sum(kblocks)
    xchunks = 1
    while xchunks * 2 <= nk and d % (xchunks * 2 * LANE) == 0:
        xchunks *= 2
    cw = d // xchunks
    row_out = pl.BlockSpec((tm, d), lambda i, k: (i, 0))
    row_out1 = pl.BlockSpec((tm, d), lambda i, k: (i, 0), pipeline_mode=pl.Buffered(1))
    gain = pl.BlockSpec((1, d), lambda i, k: (0, 0))

    def lhs_spec(start, nb):
        return pl.BlockSpec((tm, tk), lambda i, k: (i, jnp.clip(k - start, 0, nb - 1)))

    est = len(lhs) * 2 * tm * tk * 2 + 2 * tk * d * 2 + 2 * tm * cw * 4 + tm * d * 4 \
        + 2 * tm * d * 4 + n_next * tm * d * 2 + 6 * EPILOGUE_ROWS * d * 4
    outs = pl.pallas_call(
        functools.partial(_mm_resid_kernel, kblocks=kblocks, coef=coef, n_next=n_next, tm=tm,
                          xchunks=xchunks),
        out_shape=[jax.ShapeDtypeStruct((m, d), F32)]
        + [jax.ShapeDtypeStruct((m, d), BF16) for _ in range(n_next)],
        grid=(m // tm, nk),
        in_specs=[lhs_spec(s, nb) for s, nb in zip(starts, kblocks)]
        + [pl.BlockSpec((tk, d), lambda i, k: (k, 0)),
           pl.BlockSpec((tm, cw), lambda i, k: (i, jnp.minimum(k, xchunks - 1))),
           gain] + [gain] * n_next,
        out_specs=[row_out] + [row_out1] * n_next,
        scratch_shapes=[pltpu.VMEM((tm, d), F32)],
        compiler_params=_params(("parallel", "arbitrary"), est),
        name="mm_resid",
    )(*lhs, w, x, g_post.reshape(1, d).astype(F32), *[g.reshape(1, d).astype(F32) for g in g_next])
    return outs[0], outs[1:]


def _pool_kernel(u_ref, halo_ref, prev_ref, w_ref, sc_ref, o_ref, ext_ref, *, tm, pos0, group):
    i = pl.program_id(1)

    @pl.when(i == 0)
    def _():
        ext_ref[0:HALO, :] = prev_ref[0]

    @pl.when(i > 0)
    def _():
        ext_ref[0:HALO, :] = halo_ref[0]

    ext_ref[HALO:HALO + tm, :] = u_ref[0]
    pos = pos0 + i * tm + lax.broadcasted_iota(jnp.int32, (tm, 1), 0)
    for gi, win in enumerate(POOL_WINDOWS):
        sl = slice(gi * group, (gi + 1) * group)
        ug = ext_ref[HALO:HALO + tm, sl]
        wsum = ug
        for j in range(1, win):
            wsum = wsum + ext_ref[HALO - j:HALO - j + tm, sl]
        cnt = jnp.minimum(pos + 1, win).astype(F32)
        diff = wsum / cnt - ug
        y = jnp.dot(diff.astype(BF16), w_ref[gi], preferred_element_type=F32)
        o_ref[0, :, sl] = (y * sc_ref[:, sl]).astype(o_ref.dtype)


def pool_mixer(u, prev, pos0, pool_w, pool_scale):
    b, n, c = u.shape
    group = c // len(POOL_WINDOWS)
    tm = _tile(n, 256)
    steps = tm // HALO
    est = 2 * (tm * c * 4 + 2 * HALO * c * 4 + pool_w.size * 2 + tm * c * 2) + (tm + HALO) * c * 4 \
        + 4 * tm * group * 4
    return pl.pallas_call(
        functools.partial(_pool_kernel, tm=tm, pos0=pos0, group=group),
        out_shape=jax.ShapeDtypeStruct((b, n, c), BF16),
        grid=(b, n // tm),
        in_specs=[pl.BlockSpec((1, tm, c), lambda bb, i: (bb, i, 0)),
                  pl.BlockSpec((1, HALO, c), lambda bb, i: (bb, jnp.maximum(i * steps - 1, 0), 0)),
                  pl.BlockSpec((1, HALO, c), lambda bb, i: (bb, 0, 0)),
                  pl.BlockSpec(pool_w.shape, lambda bb, i: (0, 0, 0)),
                  pl.BlockSpec((1, c), lambda bb, i: (0, 0))],
        out_specs=pl.BlockSpec((1, tm, c), lambda bb, i: (bb, i, 0)),
        scratch_shapes=[pltpu.VMEM((tm + HALO, c), F32)],
        compiler_params=_params(("parallel", "parallel"), est),
        name="pool_mixer",
    )(u, u, prev, pool_w, pool_scale.reshape(1, c).astype(F32))


def _mem_attn_kernel(q_ref, k_ref, v_ref, o_ref, *, heads, hd):
    scale = hd ** -0.5
    for hh in range(heads):
        sl = slice(hh * hd, (hh + 1) * hd)
        q = q_ref[0, :, sl]
        k = k_ref[0, :, sl].astype(BF16)
        v = v_ref[0, :, sl].astype(BF16)
        s = lax.dot_general(q, k, (((1,), (1,)), ((), ())), preferred_element_type=F32) * scale
        p = jnp.exp(s - jnp.max(s, axis=-1, keepdims=True))
        p = p / jnp.sum(p, axis=-1, keepdims=True)
        o_ref[0, :, sl] = jnp.dot(p.astype(BF16), v, preferred_element_type=F32).astype(o_ref.dtype)


def mem_attend(qm, mk, mv, heads):
    b, n, dq = qm.shape
    nm = mk.shape[1]
    tm = _tile(n, 512)
    est = 2 * (2 * tm * dq * 2 + 2 * nm * dq * 4) + 4 * tm * nm * 4
    return pl.pallas_call(
        functools.partial(_mem_attn_kernel, heads=heads, hd=dq // heads),
        out_shape=jax.ShapeDtypeStruct((b, n, dq), BF16),
        grid=(b, n // tm),
        in_specs=[pl.BlockSpec((1, tm, dq), lambda bb, i: (bb, i, 0)),
                  pl.BlockSpec((1, nm, dq), lambda bb, i: (bb, 0, 0)),
                  pl.BlockSpec((1, nm, dq), lambda bb, i: (bb, 0, 0))],
        out_specs=pl.BlockSpec((1, tm, dq), lambda bb, i: (bb, i, 0)),
        compiler_params=_params(("parallel", "parallel"), est),
        name="mem_attend",
    )(qm, mk, mv)


def _logf_kernel(h_ref, w_ref, b_ref, o_ref):
    z = jnp.dot(h_ref[...], w_ref[...], preferred_element_type=F32) + b_ref[...]
    o_ref[...] = jnp.minimum(z, 0.0) - jnp.log1p(jnp.exp(-jnp.abs(z)))


def log_forget(hk, w_f, b_f):
    m, k = hk.shape
    nh = w_f.shape[1]
    wp = jnp.pad(w_f, ((0, 0), (0, LANE - nh))).astype(BF16)
    bp = jnp.pad(b_f, (0, LANE - nh)).reshape(1, LANE).astype(F32)
    tm = _tile(m, 1024)
    out = pl.pallas_call(
        _logf_kernel,
        out_shape=jax.ShapeDtypeStruct((m, LANE), F32),
        grid=(m // tm,),
        in_specs=[pl.BlockSpec((tm, k), lambda i: (i, 0)),
                  pl.BlockSpec((k, LANE), lambda i: (0, 0)),
                  pl.BlockSpec((1, LANE), lambda i: (0, 0))],
        out_specs=pl.BlockSpec((tm, LANE), lambda i: (i, 0)),
        compiler_params=_params(("parallel",), 2 * (tm * k * 2 + k * LANE * 2 + tm * LANE * 4)),
        name="log_forget",
    )(hk, wp, bp)
    return out[:, :nh]


def _cumsum_kernel(x_ref, o_ref, carry_ref, *, tb):
    @pl.when(pl.program_id(1) == 0)
    def _():
        carry_ref[...] = jnp.zeros_like(carry_ref)

    x = x_ref[0]
    row = lax.broadcasted_iota(jnp.int32, x.shape, 0)
    d = 1
    while d < tb:
        x = x + jnp.where(row >= d, pltpu.roll(x, d, 0), 0.0)
        d *= 2
    x = x + carry_ref[0:1, :]
    o_ref[0] = x
    carry_ref[...] = jnp.broadcast_to(x[tb - 1:tb, :], carry_ref.shape)


def cumsum_time(logf):
    b, t, nh = logf.shape
    tb = 256
    tp = -(-t // tb) * tb
    x = jnp.pad(logf, ((0, 0), (0, tp - t), (0, LANE - nh)))
    out = pl.pallas_call(
        functools.partial(_cumsum_kernel, tb=tb),
        out_shape=jax.ShapeDtypeStruct((b, tp, LANE), F32),
        grid=(b, tp // tb),
        in_specs=[pl.BlockSpec((1, tb, LANE), lambda bb, i: (bb, i, 0))],
        out_specs=pl.BlockSpec((1, tb, LANE), lambda bb, i: (bb, i, 0)),
        scratch_shapes=[pltpu.VMEM((8, LANE), F32)],
        compiler_params=_params(("parallel", "arbitrary"), 8 * tb * LANE * 4),
        name="cumsum_time",
    )(x)
    return out[:, :t, :nh]


LOG2E = 1.4426950408889634
_NT = (((1,), (1,)), ((), ()))


def _lanes(x, reps):
    return x if reps == 1 else jnp.concatenate([x] * reps, axis=1)


def _fox_prompt_kernel(q_ref, k_ref, v_ref, f_ref, o_ref, s0_ref, s1_ref, m_ref, acc_ref,
                       *, tq, tk, hd, nhb):
    qi = pl.program_id(1)
    m_ref[...] = jnp.full_like(m_ref, MASK_VALUE)
    acc_ref[...] = jnp.zeros_like(acc_ref)
    ones = jnp.ones((tk, hd), BF16)

    def logits(kb, s_ref):
        off = pl.multiple_of(kb * tk, tk)
        for hh in range(nhb):
            sl = slice(hh * hd, (hh + 1) * hd)
            s = lax.dot_general(q_ref[:, sl], k_ref[pl.ds(off, tk), sl], _NT,
                                preferred_element_type=F32)
            s_ref[hh] = s - f_ref[hh, kb] * LOG2E

    def attend(kb, s_ref, diag=None):
        off = pl.multiple_of(kb * tk, tk)
        for hh in range(nhb):
            sl = slice(hh * hd, (hh + 1) * hd)
            s = s_ref[hh]
            if diag is not None:
                row = lax.broadcasted_iota(jnp.int32, s.shape, 0)
                col = lax.broadcasted_iota(jnp.int32, s.shape, 1)
                s = jnp.where(col + diag * tk <= row, s, MASK_VALUE)
            m_prev = m_ref[hh]
            m_new = jnp.maximum(m_prev, jnp.max(s, axis=-1, keepdims=True))
            alpha = jnp.exp2(m_prev - m_new)
            p = jnp.exp2(s - _lanes(m_new, tk // LANE))
            v = v_ref[pl.ds(off, tk), sl]
            pv = jnp.dot(p.astype(BF16), jnp.concatenate([v, ones], axis=1),
                         preferred_element_type=F32)
            acc_ref[hh] = _lanes(alpha, 2 * hd // LANE) * acc_ref[hh] + pv
            m_ref[hh] = m_new

    def body(jj, carry):
        kb = 2 * jj
        logits(kb + 1, s1_ref)
        attend(kb, s0_ref)
        logits(kb + 2, s0_ref)
        attend(kb + 1, s1_ref)
        return carry

    logits(0, s0_ref)
    lax.fori_loop(0, qi, body, 0)
    logits(2 * qi + 1, s1_ref)
    attend(2 * qi, s0_ref, diag=0)
    attend(2 * qi + 1, s1_ref, diag=1)

    for hh in range(nhb):
        acc = acc_ref[hh]
        o_ref[:, hh * hd:(hh + 1) * hd] = (acc[:, :hd] / acc[:, hd:]).astype(o_ref.dtype)


def fox_prompt(q, k, v, f_cum, hd):
    n, dm = q.shape
    nh = dm // hd
    nhb = 2 if nh % 2 == 0 else 1
    tq = _tile(n, 1024)
    tk = tq // 2
    assert hd == LANE and tk % LANE == 0
    f_t = f_cum.T.reshape(nh, n // tk, 1, tk)
    wb = nhb * hd
    est = 2 * (2 * tq * wb * 2 + 2 * n * wb * 2 + nhb * (n // tk) * 8 * tk * 4) \
        + nhb * (tq * LANE * 4 + tq * 2 * hd * 4 + 5 * tq * tk * 4)
    return pl.pallas_call(
        functools.partial(_fox_prompt_kernel, tq=tq, tk=tk, hd=hd, nhb=nhb),
        out_shape=jax.ShapeDtypeStruct((n, dm), BF16),
        grid=(nh // nhb, n // tq),
        in_specs=[pl.BlockSpec((tq, wb), lambda h, i: (i, h)),
                  pl.BlockSpec((n, wb), lambda h, i: (0, h)),
                  pl.BlockSpec((n, wb), lambda h, i: (0, h)),
                  pl.BlockSpec((nhb, n // tk, 1, tk), lambda h, i: (h, 0, 0, 0))],
        out_specs=pl.BlockSpec((tq, wb), lambda h, i: (i, h)),
        scratch_shapes=[pltpu.VMEM((nhb, tq, tk), F32), pltpu.VMEM((nhb, tq, tk), F32),
                        pltpu.VMEM((nhb, tq, LANE), F32), pltpu.VMEM((nhb, tq, 2 * hd), F32)],
        compiler_params=_params(("parallel", "parallel"), est),
        name="fox_prompt",
    )(q, k, v, f_t)


def _fox_cached_kernel(q_ref, kp_ref, vp_ref, kn_ref, vn_ref, fp_ref, fn_ref, o_ref):
    q = q_ref[0]
    sp = lax.dot_general(q, kp_ref[0], _NT, preferred_element_type=F32) - fp_ref[0, 0] * LOG2E
    sn = lax.dot_general(q, kn_ref[0], _NT, preferred_element_type=F32) - fn_ref[0, 0] * LOG2E
    row = lax.broadcasted_iota(jnp.int32, sn.shape, 0)
    col = lax.broadcasted_iota(jnp.int32, sn.shape, 1)
    sn = jnp.where(col <= row, sn, MASK_VALUE)
    m = jnp.maximum(jnp.max(sp, axis=-1, keepdims=True), jnp.max(sn, axis=-1, keepdims=True))
    pp = jnp.exp2(sp - m)
    pn = jnp.exp2(sn - m)
    denom = jnp.sum(pp, axis=-1, keepdims=True) + jnp.sum(pn, axis=-1, keepdims=True)
    acc = jnp.dot(pp.astype(BF16), vp_ref[0], preferred_element_type=F32)
    acc = acc + jnp.dot(pn.astype(BF16), vn_ref[0], preferred_element_type=F32)
    o_ref[0] = (acc / denom).astype(o_ref.dtype)


def fox_cached(q, k_past, v_past, k_new, v_new, f_past, f_new, hd):
    b, n, dm = q.shape
    p = k_past.shape[1]
    nh = dm // hd
    fp_t = jnp.transpose(f_past, (0, 2, 1)).reshape(b, nh, 1, p)
    fn_t = jnp.transpose(f_new, (0, 2, 1)).reshape(b, nh, 1, n)
    new = pl.BlockSpec((1, n, hd), lambda bb, h: (bb, 0, h))
    past = pl.BlockSpec((1, p, hd), lambda bb, h: (bb, 0, h))
    est = 2 * (2 * p * hd * 2 + 4 * n * hd * 2) + 6 * 8 * p * 4
    return pl.pallas_call(
        _fox_cached_kernel,
        out_shape=jax.ShapeDtypeStruct((b, n, dm), BF16),
        grid=(b, nh),
        in_specs=[new, past, past, new, new,
                  pl.BlockSpec((1, 1, 1, p), lambda bb, h: (bb, h, 0, 0)),
                  pl.BlockSpec((1, 1, 1, n), lambda bb, h: (bb, h, 0, 0))],
        out_specs=new,
        compiler_params=_params(("parallel", "parallel"), est),
        name="fox_cached",
    )(q, k_past, v_past, k_new, v_new, fp_t, fn_t)


def _trunk(x, pool_prev, mem_k, mem_v, past, w):
    b, n, d = x.shape
    m = b * n
    depth, n_a = w["depth"], w["n_a"]
    c, hd, heads = w["d_pool"], w["hd_fox"], w["mem_heads"]
    assert n >= POOL_STATE
    pos0 = 0 if past is None else past[0].shape[1]
    x2 = x.reshape(m, d)
    h = rms_cast(x2, w["norm_g"][0, 0])
    new_pool = []
    k_new = v_new = logf_new = None
    k_bf = v_bf = f_all = None
    for l in range(depth):
        g = w["norm_g"][l]
        a = ffn_up(h, w["wg"][l][0], w["wu"][l][0])
        x2, (h,) = mm_resid([a], w["wd"][l][0], x2, g[1], 0.5, [g[2]])
        if l < n_a:
            w_in, w_out = w["w_in_a"][l], w["w_out_a"][l]
            (u,) = mm(h, w_in[:, :c], [F32])
            u3 = u.reshape(b, n, c)
            main = pool_mixer(u3, pool_prev[l], pos0, w["pool_w"][l], w["pool_scale"][l])
            new_pool.append(u3[:, n - POOL_STATE:, :])
        else:
            j = l - n_a
            w_in, w_out = w["w_in_b"][j], w["w_out_b"][j]
            (q,) = mm(h, w_in[:, :c], [BF16], scale=LOG2E * hd ** -0.5)
            if past is None:
                main = fox_prompt(q, k_bf, v_bf, f_all[0], hd).reshape(b, n, c)
            else:
                p = pos0
                main = fox_cached(q.reshape(b, n, c), past[0], past[1], k_bf.reshape(b, n, c),
                                  v_bf.reshape(b, n, c), f_all[:, :p], f_all[:, p:], hd)
        (qm,) = mm(h, w_in[:, c:], [BF16])
        mem = mem_attend(qm.reshape(b, n, d - c), mem_k[l], mem_v[l], heads)
        x2, (h,) = mm_resid([main.reshape(m, c), mem.reshape(m, d - c)], w_out, x2, g[3], 1.0, [g[4]])
        a = ffn_up(h, w["wg"][l][1], w["wu"][l][1])
        g_next = []
        if l + 1 < depth:
            g_next.append(w["norm_g"][l + 1, 0])
        if l == n_a - 1:
            g_next.append(w["g_kv"])
        x2, hs = mm_resid([a], w["wd"][l][1], x2, g[5], 0.5, g_next)
        if l + 1 < depth:
            h = hs[0]
        if l == n_a - 1:
            hk = hs[-1]
            k_new, k_bf = mm(hk, w["w_kv"][:, :c], [F32, BF16])
            v_new, v_bf = mm(hk, w["w_kv"][:, c:], [F32, BF16])
            logf_new = log_forget(hk, w["w_f"], w["b_f"]).reshape(b, n, -1)
            logf_all = logf_new if past is None else jnp.concatenate(
                [past[2].astype(F32), logf_new], axis=1)
            f_all = cumsum_time(logf_all)
    nh = c // hd
    return (x2.reshape(b, n, d), jnp.stack(new_pool), k_new.reshape(b, n, nh, hd),
            v_new.reshape(b, n, nh, hd), logf_new)


def _cast_pad_kernel(x_ref, o_ref, *, rows, cols):
    tr, tc = o_ref.shape
    r = pl.program_id(0) * tr + lax.broadcasted_iota(jnp.int32, (tr, tc), 0)
    c = pl.program_id(1) * tc + lax.broadcasted_iota(jnp.int32, (tr, tc), 1)
    o_ref[...] = jnp.where((r < rows) & (c < cols), x_ref[...], 0.0).astype(o_ref.dtype)


def _pad_ff(w4, l, s, axis):
    rows, cols = w4.shape[2:]
    shape = [rows, cols]
    shape[axis] += -shape[axis] % FF_ALIGN
    tr, tc = _tile(shape[0], 1024), _tile(shape[1], 1024)
    return pl.pallas_call(
        functools.partial(_cast_pad_kernel, rows=rows, cols=cols),
        out_shape=jax.ShapeDtypeStruct(tuple(shape), BF16),
        grid=(shape[0] // tr, shape[1] // tc),
        in_specs=[pl.BlockSpec((None, None, tr, tc), lambda i, j: (l, s, i, j))],
        out_specs=pl.BlockSpec((tr, tc), lambda i, j: (i, j)),
        compiler_params=_params(("parallel", "parallel"), 2 * tr * tc * 6 + 3 * tr * tc * 4),
        name="cast_pad",
    )(w4)


def kernel(x_prompt, x_sample, mem_prompt, cache_fox_k, cache_fox_v, cache_fox_logf, cache_mem_k, cache_mem_v, state_pool, norm_g, w_ffn_gate, w_ffn_up, w_ffn_down, w_in_a, pool_w, pool_scale, w_out_a, w_in_b, w_out_b, g_kv, w_kv, w_f, b_f, g_mem, w_mem_kv):
    depth = norm_g.shape[0]
    n_a, db, _, c = state_pool.shape
    _, p, nh, hd = cache_fox_k.shape
    _, _, n_mem, heads, mem_hd = cache_mem_k.shape
    dq = heads * mem_hd
    bp, _, d = x_prompt.shape
    w = dict(
        depth=depth, n_a=n_a, d_pool=c, hd_fox=hd, mem_heads=heads,
        norm_g=norm_g, g_kv=g_kv, w_f=w_f, b_f=b_f, pool_scale=pool_scale,
        wg=[[_pad_ff(w_ffn_gate, l, s, 1) for s in range(2)] for l in range(depth)],
        wu=[[_pad_ff(w_ffn_up, l, s, 1) for s in range(2)] for l in range(depth)],
        wd=[[_pad_ff(w_ffn_down, l, s, 0) for s in range(2)] for l in range(depth)],
        w_in_a=w_in_a.astype(BF16), w_out_a=w_out_a.astype(BF16),
        w_in_b=w_in_b.astype(BF16), w_out_b=w_out_b.astype(BF16),
        pool_w=pool_w.astype(BF16), w_kv=w_kv.astype(BF16),
    )
    mem2 = mem_prompt.reshape(bp * n_mem, d)
    mem_kv = [mm(rms_cast(mem2, g_mem[l]), w_mem_kv[l].astype(BF16), [F32])[0] for l in range(depth)]
    mem_k_p = jnp.stack([kv[:, :dq].reshape(bp, n_mem, dq) for kv in mem_kv])
    mem_v_p = jnp.stack([kv[:, dq:].reshape(bp, n_mem, dq) for kv in mem_kv])

    zero_prev = jnp.zeros((n_a, bp, HALO, c), F32)
    y_p, pool_p, k_p, v_p, logf_p = _trunk(x_prompt, zero_prev, mem_k_p, mem_v_p, None, w)

    prev_s = jnp.pad(state_pool.astype(F32), ((0, 0), (0, 0), (1, 0), (0, 0)))
    past = (cache_fox_k.reshape(db, p, nh * hd).astype(BF16),
            cache_fox_v.reshape(db, p, nh * hd).astype(BF16), cache_fox_logf)
    y_s, pool_s, k_s, v_s, logf_s = _trunk(
        x_sample, prev_s, cache_mem_k.reshape(depth, db, n_mem, dq),
        cache_mem_v.reshape(depth, db, n_mem, dq), past, w)

    return (y_p, y_s, k_p, v_p, logf_p,
            mem_k_p.reshape(depth, bp, n_mem, heads, mem_hd),
            mem_v_p.reshape(depth, bp, n_mem, heads, mem_hd),
            pool_p, k_s, v_s, logf_s, pool_s)
```

```python
import functools

import jax
import jax.numpy as jnp
from jax import lax
from jax.experimental import pallas as pl
from jax.experimental.pallas import tpu as pltpu

EPS = 1e-6
POOL_WINDOWS = (2, 4, 8, 16)
POOL_STATE = max(POOL_WINDOWS) - 1
HALO = POOL_STATE + 1
LANE = 128
FF_ALIGN = 1024
MASK_VALUE = -1e30
VMEM_CAP = 60 * 1024 * 1024
F32 = jnp.float32
BF16 = jnp.bfloat16


def _tile(n, pref):
    if n <= pref:
        return n
    t = pref
    while n % t:
        t //= 2
    return t


def _params(sem, est_bytes):
    limit = int(min(VMEM_CAP, max(32 * 1024 * 1024, est_bytes * 5 // 4 + (4 << 20))))
    return pltpu.CompilerParams(dimension_semantics=sem, vmem_limit_bytes=limit)


def _rms_normalize(xf):
    return xf * lax.rsqrt(jnp.mean(xf * xf, axis=-1, keepdims=True) + EPS)


def _rms_cast_kernel(x_ref, g_ref, o_ref):
    o_ref[...] = (_rms_normalize(x_ref[...]) * g_ref[...]).astype(o_ref.dtype)


def rms_cast(x, g):
    m, d = x.shape
    tm = _tile(m, 512)
    return pl.pallas_call(
        _rms_cast_kernel,
        out_shape=jax.ShapeDtypeStruct((m, d), BF16),
        grid=(m // tm,),
        in_specs=[pl.BlockSpec((tm, d), lambda i: (i, 0)),
                  pl.BlockSpec((1, d), lambda i: (0, 0))],
        out_specs=pl.BlockSpec((tm, d), lambda i: (i, 0)),
        compiler_params=_params(("parallel",), 2 * tm * d * 6),
        name="rms_cast",
    )(x, g.reshape(1, d).astype(F32))


def _mm_kernel(h_ref, w_ref, *o_refs, scale):
    acc = jnp.dot(h_ref[...], w_ref[...], preferred_element_type=F32)
    if scale is not None:
        acc = acc * scale
    for o_ref in o_refs:
        o_ref[...] = acc.astype(o_ref.dtype)


def mm(h, w, out_dtypes, scale=None, cols=None):
    m, k = h.shape
    c0, c1 = cols if cols is not None else (0, w.shape[1])
    n = c1 - c0
    tm, tn = _tile(m, 1024), _tile(n, 1024)
    assert c0 % tn == 0
    j0 = c0 // tn
    est = 2 * (tm * k * 2 + k * tn * 2) + tm * tn * 4 + sum(
        2 * tm * tn * jnp.dtype(t).itemsize for t in out_dtypes)
    outs = pl.pallas_call(
        functools.partial(_mm_kernel, scale=scale),
        out_shape=[jax.ShapeDtypeStruct((m, n), t) for t in out_dtypes],
        grid=(m // tm, n // tn),
        in_specs=[pl.BlockSpec((tm, k), lambda i, j: (i, 0)),
                  pl.BlockSpec((k, tn), lambda i, j: (0, j + j0))],
        out_specs=[pl.BlockSpec((tm, tn), lambda i, j: (i, j)) for _ in out_dtypes],
        compiler_params=_params(("parallel", "parallel"), est),
        name="mm",
    )(h, w)
    return outs


def _ffn_up_kernel(h_ref, wg_ref, wu_ref, o_ref):
    h = h_ref[...]
    g = jnp.dot(h, wg_ref[...], preferred_element_type=F32)
    u = jnp.dot(h, wu_ref[...], preferred_element_type=F32)
    o_ref[...] = (g * jax.nn.sigmoid(g) * u).astype(o_ref.dtype)


def ffn_up(h, wg, wu):
    m, k = h.shape
    n = wg.shape[1]
    tm, tn = _tile(m, 1024), _tile(n, 512)
    est = 2 * (tm * k * 2 + 2 * k * tn * 2 + tm * tn * 2) + 3 * tm * tn * 4
    return pl.pallas_call(
        _ffn_up_kernel,
        out_shape=jax.ShapeDtypeStruct((m, n), BF16),
        grid=(m // tm, n // tn),
        in_specs=[pl.BlockSpec((tm, k), lambda i, j: (i, 0)),
                  pl.BlockSpec((k, tn), lambda i, j: (0, j)),
                  pl.BlockSpec((k, tn), lambda i, j: (0, j))],
        out_specs=pl.BlockSpec((tm, tn), lambda i, j: (i, j)),
        compiler_params=_params(("parallel", "parallel"), est),
        name="ffn_up",
    )(h, wg, wu)


EPILOGUE_ROWS = 128


def _mm_resid_kernel(*refs, kblocks, coef, n_next, tm, xchunks):
    n_lhs = len(kblocks)
    a_refs = refs[:n_lhs]
    w_ref, xc_ref, gp_ref = refs[n_lhs:n_lhs + 3]
    gn_refs = refs[n_lhs + 3:n_lhs + 3 + n_next]
    xo_ref = refs[n_lhs + 3 + n_next]
    h_refs = refs[n_lhs + 4 + n_next:-1]
    x_ref = refs[-1]
    k = pl.program_id(1)
    nk = sum(kblocks)
    cw = xc_ref.shape[1]

    for c in range(xchunks):
        @pl.when(k == c)
        def _(c=c):
            x_ref[:, c * cw:(c + 1) * cw] = xc_ref[...]

    @pl.when(k == 0)
    def _():
        xo_ref[...] = jnp.zeros_like(xo_ref)

    if n_lhs == 1:
        xo_ref[...] += jnp.dot(a_refs[0][...], w_ref[...], preferred_element_type=F32)
    else:
        start = 0
        for a_ref, nb in zip(a_refs, kblocks):
            @pl.when((k >= start) & (k < start + nb))
            def _(a_ref=a_ref):
                xo_ref[...] += jnp.dot(a_ref[...], w_ref[...], preferred_element_type=F32)
            start += nb

    @pl.when(k == nk - 1)
    def _():
        rows = min(EPILOGUE_ROWS, tm)
        for r0 in range(0, tm, rows):
            sl = slice(r0, r0 + rows)
            xn = x_ref[sl, :] + coef * (_rms_normalize(xo_ref[sl, :]) * gp_ref[...])
            xo_ref[sl, :] = xn
            if n_next:
                r = _rms_normalize(xn)
                for gn_ref, h_ref in zip(gn_refs, h_refs):
                    h_ref[sl, :] = (r * gn_ref[...]).astype(h_ref.dtype)


def mm_resid(lhs, w, x, g_post, coef, g_next):
    m, d = x.shape
    widths = [a.shape[1] for a in lhs]
    assert sum(widths) == w.shape[0] and w.shape[1] == d
    tk = 1024
    while any(kw % tk for kw in widths):
        tk //= 2
    tm = _tile(m, 512)
    kblocks = tuple(kw // tk for kw in widths)
    starts = [sum(kblocks[:j]) for j in range(len(lhs))]
    n_next = len(g_next)
    nk = sum(kblocks)
    xchunks = 1
    while xchunks * 2 <= nk and d % (xchunks * 2 * LANE) == 0:
        xchunks *= 2
    cw = d // xchunks
    row_out = pl.BlockSpec((tm, d), lambda i, k: (i, 0))
    row_out1 = pl.BlockSpec((tm, d), lambda i, k: (i, 0), pipeline_mode=pl.Buffered(1))
    gain = pl.BlockSpec((1, d), lambda i, k: (0, 0))

    def lhs_spec(start, nb):
        return pl.BlockSpec((tm, tk), lambda i, k: (i, jnp.clip(k - start, 0, nb - 1)))

    est = len(lhs) * 2 * tm * tk * 2 + 2 * tk * d * 2 + 2 * tm * cw * 4 + tm * d * 4 \
        + 2 * tm * d * 4 + n_next * tm * d * 2 + 6 * EPILOGUE_ROWS * d * 4
    outs = pl.pallas_call(
        functools.partial(_mm_resid_kernel, kblocks=kblocks, coef=coef, n_next=n_next, tm=tm,
                          xchunks=xchunks),
        out_shape=[jax.ShapeDtypeStruct((m, d), F32)]
        + [jax.ShapeDtypeStruct((m, d), BF16) for _ in range(n_next)],
        grid=(m // tm, nk),
        in_specs=[lhs_spec(s, nb) for s, nb in zip(starts, kblocks)]
        + [pl.BlockSpec((tk, d), lambda i, k: (k, 0)),
           pl.BlockSpec((tm, cw), lambda i, k: (i, jnp.minimum(k, xchunks - 1))),
           gain] + [gain] * n_next,
        out_specs=[row_out] + [row_out1] * n_next,
        scratch_shapes=[pltpu.VMEM((tm, d), F32)],
        compiler_params=_params(("parallel", "arbitrary"), est),
        name="mm_resid",
    )(*lhs, w, x, g_post.reshape(1, d).astype(F32), *[g.reshape(1, d).astype(F32) for g in g_next])
    return outs[0], outs[1:]


def _pool_kernel(u_ref, halo_ref, prev_ref, w_ref, sc_ref, o_ref, ext_ref, *, tm, pos0, group):
    i = pl.program_id(1)

    @pl.when(i == 0)
    def _():
        ext_ref[0:HALO, :] = prev_ref[0]

    @pl.when(i > 0)
    def _():
        ext_ref[0:HALO, :] = halo_ref[0]

    ext_ref[HALO:HALO + tm, :] = u_ref[0]
    pos = pos0 + i * tm + lax.broadcasted_iota(jnp.int32, (tm, 1), 0)
    for gi, win in enumerate(POOL_WINDOWS):
        sl = slice(gi * group, (gi + 1) * group)
        ug = ext_ref[HALO:HALO + tm, sl]
        wsum = ug
        for j in range(1, win):
            wsum = wsum + ext_ref[HALO - j:HALO - j + tm, sl]
        cnt = jnp.minimum(pos + 1, win).astype(F32)
        diff = wsum / cnt - ug
        y = jnp.dot(diff.astype(BF16), w_ref[gi], preferred_element_type=F32)
        o_ref[0, :, sl] = (y * sc_ref[:, sl]).astype(o_ref.dtype)


def pool_mixer(u, prev, pos0, pool_w, pool_scale):
    b, n, c = u.shape
    group = c // len(POOL_WINDOWS)
    tm = _tile(n, 256)
    steps = tm // HALO
    est = 2 * (tm * c * 4 + 2 * HALO * c * 4 + pool_w.size * 2 + tm * c * 2) + (tm + HALO) * c * 4 \
        + 4 * tm * group * 4
    return pl.pallas_call(
        functools.partial(_pool_kernel, tm=tm, pos0=pos0, group=group),
        out_shape=jax.ShapeDtypeStruct((b, n, c), BF16),
        grid=(b, n // tm),
        in_specs=[pl.BlockSpec((1, tm, c), lambda bb, i: (bb, i, 0)),
                  pl.BlockSpec((1, HALO, c), lambda bb, i: (bb, jnp.maximum(i * steps - 1, 0), 0)),
                  pl.BlockSpec((1, HALO, c), lambda bb, i: (bb, 0, 0)),
                  pl.BlockSpec(pool_w.shape, lambda bb, i: (0, 0, 0)),
                  pl.BlockSpec((1, c), lambda bb, i: (0, 0))],
        out_specs=pl.BlockSpec((1, tm, c), lambda bb, i: (bb, i, 0)),
        scratch_shapes=[pltpu.VMEM((tm + HALO, c), F32)],
        compiler_params=_params(("parallel", "parallel"), est),
        name="pool_mixer",
    )(u, u, prev, pool_w, pool_scale.reshape(1, c).astype(F32))


def _mem_attn_kernel(q_ref, k_ref, v_ref, o_ref, *, heads, hd):
    scale = hd ** -0.5
    for hh in range(heads):
        sl = slice(hh * hd, (hh + 1) * hd)
        q = q_ref[0, :, sl]
        k = k_ref[0, :, sl].astype(BF16)
        v = v_ref[0, :, sl].astype(BF16)
        s = lax.dot_general(q, k, (((1,), (1,)), ((), ())), preferred_element_type=F32) * scale
        p = jnp.exp(s - jnp.max(s, axis=-1, keepdims=True))
        p = p / jnp.sum(p, axis=-1, keepdims=True)
        o_ref[0, :, sl] = jnp.dot(p.astype(BF16), v, preferred_element_type=F32).astype(o_ref.dtype)


def mem_attend(qm, mk, mv, heads):
    b, n, dq = qm.shape
    nm = mk.shape[1]
    tm = _tile(n, 512)
    est = 2 * (2 * tm * dq * 2 + 2 * nm * dq * 4) + 4 * tm * nm * 4
    return pl.pallas_call(
        functools.partial(_mem_attn_kernel, heads=heads, hd=dq // heads),
        out_shape=jax.ShapeDtypeStruct((b, n, dq), BF16),
        grid=(b, n // tm),
        in_specs=[pl.BlockSpec((1, tm, dq), lambda bb, i: (bb, i, 0)),
                  pl.BlockSpec((1, nm, dq), lambda bb, i: (bb, 0, 0)),
                  pl.BlockSpec((1, nm, dq), lambda bb, i: (bb, 0, 0))],
        out_specs=pl.BlockSpec((1, tm, dq), lambda bb, i: (bb, i, 0)),
        compiler_params=_params(("parallel", "parallel"), est),
        name="mem_attend",
    )(qm, mk, mv)


def _logf_kernel(h_ref, w_ref, b_ref, o_ref):
    z = jnp.dot(h_ref[...], w_ref[...], preferred_element_type=F32) + b_ref[...]
    o_ref[...] = jnp.minimum(z, 0.0) - jnp.log1p(jnp.exp(-jnp.abs(z)))


def log_forget(hk, w_f, b_f):
    m, k = hk.shape
    nh = w_f.shape[1]
    wp = jnp.pad(w_f, ((0, 0), (0, LANE - nh))).astype(BF16)
    bp = jnp.pad(b_f, (0, LANE - nh)).reshape(1, LANE).astype(F32)
    tm = _tile(m, 1024)
    out = pl.pallas_call(
        _logf_kernel,
        out_shape=jax.ShapeDtypeStruct((m, LANE), F32),
        grid=(m // tm,),
        in_specs=[pl.BlockSpec((tm, k), lambda i: (i, 0)),
                  pl.BlockSpec((k, LANE), lambda i: (0, 0)),
                  pl.BlockSpec((1, LANE), lambda i: (0, 0))],
        out_specs=pl.BlockSpec((tm, LANE), lambda i: (i, 0)),
        compiler_params=_params(("parallel",), 2 * (tm * k * 2 + k * LANE * 2 + tm * LANE * 4)),
        name="log_forget",
    )(hk, wp, bp)
    return out[:, :nh]


def _cumsum_kernel(x_ref, o_ref, carry_ref, *, tb):
    @pl.when(pl.program_id(1) == 0)
    def _():
        carry_ref[...] = jnp.zeros_like(carry_ref)

    x = x_ref[0]
    row = lax.broadcasted_iota(jnp.int32, x.shape, 0)
    d = 1
    while d < tb:
        x = x + jnp.where(row >= d, pltpu.roll(x, d, 0), 0.0)
        d *= 2
    x = x + carry_ref[0:1, :]
    o_ref[0] = x
    carry_ref[...] = jnp.broadcast_to(x[tb - 1:tb, :], carry_ref.shape)


def cumsum_time(logf):
    b, t, nh = logf.shape
    tb = 256
    tp = -(-t // tb) * tb
    x = jnp.pad(logf, ((0, 0), (0, tp - t), (0, LANE - nh)))
    out = pl.pallas_call(
        functools.partial(_cumsum_kernel, tb=tb),
        out_shape=jax.ShapeDtypeStruct((b, tp, LANE), F32),
        grid=(b, tp // tb),
        in_specs=[pl.BlockSpec((1, tb, LANE), lambda bb, i: (bb, i, 0))],
        out_specs=pl.BlockSpec((1, tb, LANE), lambda bb, i: (bb, i, 0)),
        scratch_shapes=[pltpu.VMEM((8, LANE), F32)],
        compiler_params=_params(("parallel", "arbitrary"), 8 * tb * LANE * 4),
        name="cumsum_time",
    )(x)
    return out[:, :t, :nh]


LOG2E = 1.4426950408889634
_NT = (((1,), (1,)), ((), ()))


def _lanes(x, reps):
    return x if reps == 1 else jnp.concatenate([x] * reps, axis=1)


def _fox_prompt_kernel(q_ref, k_ref, v_ref, f_ref, o_ref, s0_ref, s1_ref, m_ref, acc_ref,
                       *, tq, tk, hd, nhb):
    qi = pl.program_id(1)
    m_ref[...] = jnp.full_like(m_ref, MASK_VALUE)
    acc_ref[...] = jnp.zeros_like(acc_ref)
    ones = jnp.ones((tk, hd), BF16)

    def logits(kb, s_ref):
        off = pl.multiple_of(kb * tk, tk)
        for hh in range(nhb):
            sl = slice(hh * hd, (hh + 1) * hd)
            s = lax.dot_general(q_ref[:, sl], k_ref[pl.ds(off, tk), sl], _NT,
                                preferred_element_type=F32)
            s_ref[hh] = s - f_ref[hh, kb] * LOG2E

    def attend(kb, s_ref, diag=None):
        off = pl.multiple_of(kb * tk, tk)
        for hh in range(nhb):
            sl = slice(hh * hd, (hh + 1) * hd)
            s = s_ref[hh]
            if diag is not None:
                row = lax.broadcasted_iota(jnp.int32, s.shape, 0)
                col = lax.broadcasted_iota(jnp.int32, s.shape, 1)
                s = jnp.where(col + diag * tk <= row, s, MASK_VALUE)
            m_prev = m_ref[hh]
            m_new = jnp.maximum(m_prev, jnp.max(s, axis=-1, keepdims=True))
            alpha = jnp.exp2(m_prev - m_new)
            p = jnp.exp2(s - _lanes(m_new, tk // LANE))
            v = v_ref[pl.ds(off, tk), sl]
            pv = jnp.dot(p.astype(BF16), jnp.concatenate([v, ones], axis=1),
                         preferred_element_type=F32)
            acc_ref[hh] = _lanes(alpha, 2 * hd // LANE) * acc_ref[hh] + pv
            m_ref[hh] = m_new

    def body(jj, carry):
        kb = 2 * jj
        logits(kb + 1, s1_ref)
        attend(kb, s0_ref)
        logits(kb + 2, s0_ref)
        attend(kb + 1, s1_ref)
        return carry

    logits(0, s0_ref)
    lax.fori_loop(0, qi, body, 0)
    logits(2 * qi + 1, s1_ref)
    attend(2 * qi, s0_ref, diag=0)
    attend(2 * qi + 1, s1_ref, diag=1)

    for hh in range(nhb):
        acc = acc_ref[hh]
        o_ref[:, hh * hd:(hh + 1) * hd] = (acc[:, :hd] / acc[:, hd:]).astype(o_ref.dtype)


def fox_prompt(q, k, v, f_cum, hd):
    n, dm = q.shape
    nh = dm // hd
    nhb = 2 if nh % 2 == 0 else 1
    tq = _tile(n, 1024)
    tk = tq // 2
    assert hd == LANE and tk % LANE == 0
    f_t = f_cum.T.reshape(nh, n // tk, 1, tk)
    wb = nhb * hd
    est = 2 * (2 * tq * wb * 2 + 2 * n * wb * 2 + nhb * (n // tk) * 8 * tk * 4) \
        + nhb * (tq * LANE * 4 + tq * 2 * hd * 4 + 5 * tq * tk * 4)
    return pl.pallas_call(
        functools.partial(_fox_prompt_kernel, tq=tq, tk=tk, hd=hd, nhb=nhb),
        out_shape=jax.ShapeDtypeStruct((n, dm), BF16),
        grid=(nh // nhb, n // tq),
        in_specs=[pl.BlockSpec((tq, wb), lambda h, i: (i, h)),
                  pl.BlockSpec((n, wb), lambda h, i: (0, h)),
                  pl.BlockSpec((n, wb), lambda h, i: (0, h)),
                  pl.BlockSpec((nhb, n // tk, 1, tk), lambda h, i: (h, 0, 0, 0))],
        out_specs=pl.BlockSpec((tq, wb), lambda h, i: (i, h)),
        scratch_shapes=[pltpu.VMEM((nhb, tq, tk), F32), pltpu.VMEM((nhb, tq, tk), F32),
                        pltpu.VMEM((nhb, tq, LANE), F32), pltpu.VMEM((nhb, tq, 2 * hd), F32)],
        compiler_params=_params(("parallel", "parallel"), est),
        name="fox_prompt",
    )(q, k, v, f_t)


def _fox_cached_kernel(q_ref, kp_ref, vp_ref, kn_ref, vn_ref, fp_ref, fn_ref, rh_ref, rt_ref,
                       ch_ref, ct_ref, o_ref, m_ref, acc_ref, *, sub, hd):
    j = pl.program_id(1)
    q = q_ref[0]
    rh = rh_ref[...]

    @pl.when(j == 0)
    def _():
        m_ref[...] = jnp.full_like(m_ref, MASK_VALUE)
        acc_ref[...] = jnp.zeros_like(acc_ref)

    def logits(k, f, valid):
        s = lax.dot_general(q, k, _NT, preferred_element_type=F32) - f * LOG2E
        return jnp.where(valid, s, MASK_VALUE)

    def absorb(s, v):
        width = s.shape[1]
        m_prev = m_ref[...]
        m_new = jnp.maximum(m_prev, jnp.max(s, axis=-1, keepdims=True))
        alpha = jnp.exp2(m_prev - m_new)
        p = jnp.exp2(s - (_lanes(m_new, width // LANE) if width % LANE == 0 else m_new[:, :1]))
        vaug = jnp.concatenate([v, jnp.ones_like(v)], axis=1)
        pv = jnp.dot(p.astype(BF16), vaug, preferred_element_type=F32)
        acc_ref[...] = _lanes(alpha, 2 * hd // LANE) * acc_ref[...] + pv
        m_ref[...] = m_new

    subs = [slice(c0, c0 + sub) for c0 in range(0, kp_ref.shape[1], sub)]
    s_all = [logits(kp_ref[0, cs, :].astype(BF16), fp_ref[0, :, cs], rh == ch_ref[:, cs])
             for cs in subs]
    for s, cs in zip(s_all, subs):
        absorb(s, vp_ref[0, cs, :].astype(BF16))

    @pl.when(j == pl.num_programs(1) - 1)
    def _():
        r = kn_ref.shape[1]
        absorb(logits(kn_ref[0], fn_ref[0], (rh == ch_ref[:, :r]) & (ct_ref[...] <= rt_ref[...])),
               vn_ref[0])
        acc = acc_ref[...]
        o_ref[0] = (acc[:, :hd] / acc[:, hd:]).astype(o_ref.dtype)


def fox_cached(q, k_past, v_past, k_new, v_new, f_all, hd):
    b, n, dm = q.shape
    p, nh = k_past.shape[1:3]
    r = n * nh
    tp = _tile(p, 256)
    c = tp * nh
    sub = c // 2 if c % (2 * LANE * nh) == 0 else c
    assert c % LANE == 0 and c >= r
    heads = jnp.arange(c, dtype=jnp.int32) % nh
    times = jnp.arange(r, dtype=jnp.int32) // nh
    new = pl.BlockSpec((1, r, hd), lambda bb, j: (bb, 0, 0))
    past = pl.BlockSpec((1, c, hd), lambda bb, j: (bb, j, 0))
    whole = lambda shape: pl.BlockSpec(shape, lambda bb, j: (0, 0))
    est = 2 * (2 * c * hd * 4 + 4 * r * hd * 2) + 6 * r * sub * 4 + 3 * r * 2 * hd * 4
    out = pl.pallas_call(
        functools.partial(_fox_cached_kernel, sub=sub, hd=hd),
        out_shape=jax.ShapeDtypeStruct((b, r, hd), BF16),
        grid=(b, p // tp),
        in_specs=[new, past, past, new, new,
                  pl.BlockSpec((1, 1, c), lambda bb, j: (bb, 0, j)),
                  pl.BlockSpec((1, 1, r), lambda bb, j: (bb, 0, 0)),
                  whole((r, 1)), whole((r, 1)), whole((1, c)), whole((1, r))],
        out_specs=new,
        scratch_shapes=[pltpu.VMEM((r, LANE), F32), pltpu.VMEM((r, 2 * hd), F32)],
        compiler_params=_params(("parallel", "arbitrary"), est),
        name="fox_cached",
    )(q.reshape(b, r, hd), k_past.reshape(b, p * nh, hd), v_past.reshape(b, p * nh, hd),
      k_new.reshape(b, r, hd), v_new.reshape(b, r, hd),
      f_all[:, :p].reshape(b, 1, p * nh), f_all[:, p:].reshape(b, 1, r),
      heads[:r].reshape(r, 1), times.reshape(r, 1), heads.reshape(1, c), times.reshape(1, r))
    return out.reshape(b, n, dm)


def _trunk(x, pool_prev, mem_k, mem_v, past, w):
    b, n, d = x.shape
    m = b * n
    depth, n_a = w["depth"], w["n_a"]
    c, hd, heads = w["d_pool"], w["hd_fox"], w["mem_heads"]
    assert n >= POOL_STATE
    pos0 = 0 if past is None else past[0].shape[1]
    x2 = x.reshape(m, d)
    h = rms_cast(x2, w["norm_g"][0, 0])
    new_pool = []
    k_new = v_new = logf_new = None
    k_bf = v_bf = f_all = None
    for l in range(depth):
        g = w["norm_g"][l]
        a = ffn_up(h, w["wg"][l][0], w["wu"][l][0])
        x2, (h,) = mm_resid([a], w["wd"][l][0], x2, g[1], 0.5, [g[2]])
        if l < n_a:
            w_in, w_out = w["w_in_a"][l], w["w_out_a"][l]
            (u,) = mm(h, w_in, [F32], cols=(0, c))
            u3 = u.reshape(b, n, c)
            main = pool_mixer(u3, pool_prev[l], pos0, w["pool_w"][l], w["pool_scale"][l])
            new_pool.append(u3[:, n - POOL_STATE:, :])
        else:
            j = l - n_a
            w_in, w_out = w["w_in_b"][j], w["w_out_b"][j]
            (q,) = mm(h, w_in, [BF16], scale=LOG2E * hd ** -0.5, cols=(0, c))
            if past is None:
                main = fox_prompt(q, k_bf, v_bf, f_all[0], hd).reshape(b, n, c)
            else:
                main = fox_cached(q.reshape(b, n, c), past[0], past[1], k_bf.reshape(b, n, c),
                                  v_bf.reshape(b, n, c), f_all, hd)
        (qm,) = mm(h, w_in, [BF16], cols=(c, d))
        mem = mem_attend(qm.reshape(b, n, d - c), mem_k[l], mem_v[l], heads)
        x2, (h,) = mm_resid([main.reshape(m, c), mem.reshape(m, d - c)], w_out, x2, g[3], 1.0, [g[4]])
        a = ffn_up(h, w["wg"][l][1], w["wu"][l][1])
        g_next = []
        if l + 1 < depth:
            g_next.append(w["norm_g"][l + 1, 0])
        if l == n_a - 1:
            g_next.append(w["g_kv"])
        x2, hs = mm_resid([a], w["wd"][l][1], x2, g[5], 0.5, g_next)
        if l + 1 < depth:
            h = hs[0]
        if l == n_a - 1:
            hk = hs[-1]
            k_new, k_bf = mm(hk, w["w_kv"], [F32, BF16], cols=(0, c))
            v_new, v_bf = mm(hk, w["w_kv"], [F32, BF16], cols=(c, 2 * c))
            logf_new = log_forget(hk, w["w_f"], w["b_f"]).reshape(b, n, -1)
            logf_all = logf_new if past is None else jnp.concatenate(
                [past[2].astype(F32), logf_new], axis=1)
            f_all = cumsum_time(logf_all)
    nh = c // hd
    return (x2.reshape(b, n, d), jnp.stack(new_pool), k_new.reshape(b, n, nh, hd),
            v_new.reshape(b, n, nh, hd), logf_new)


def _cast_pad_kernel(x_ref, o_ref, *, rows, cols):
    tr, tc = o_ref.shape
    r = pl.program_id(0) * tr + lax.broadcasted_iota(jnp.int32, (tr, tc), 0)
    c = pl.program_id(1) * tc + lax.broadcasted_iota(jnp.int32, (tr, tc), 1)
    o_ref[...] = jnp.where((r < rows) & (c < cols), x_ref[...], 0.0).astype(o_ref.dtype)


def _pad_ff(w4, l, s, axis):
    rows, cols = w4.shape[2:]
    shape = [rows, cols]
    shape[axis] += -shape[axis] % FF_ALIGN
    tr, tc = _tile(shape[0], 1024), _tile(shape[1], 1024)
    return pl.pallas_call(
        functools.partial(_cast_pad_kernel, rows=rows, cols=cols),
        out_shape=jax.ShapeDtypeStruct(tuple(shape), BF16),
        grid=(shape[0] // tr, shape[1] // tc),
        in_specs=[pl.BlockSpec((None, None, tr, tc), lambda i, j: (l, s, i, j))],
        out_specs=pl.BlockSpec((tr, tc), lambda i, j: (i, j)),
        compiler_params=_params(("parallel", "parallel"), 2 * tr * tc * 6 + 3 * tr * tc * 4),
        name="cast_pad",
    )(w4)


def kernel(x_prompt, x_sample, mem_prompt, cache_fox_k, cache_fox_v, cache_fox_logf, cache_mem_k, cache_mem_v, state_pool, norm_g, w_ffn_gate, w_ffn_up, w_ffn_down, w_in_a, pool_w, pool_scale, w_out_a, w_in_b, w_out_b, g_kv, w_kv, w_f, b_f, g_mem, w_mem_kv):
    depth = norm_g.shape[0]
    n_a, db, _, c = state_pool.shape
    _, p, nh, hd = cache_fox_k.shape
    _, _, n_mem, heads, mem_hd = cache_mem_k.shape
    dq = heads * mem_hd
    bp, _, d = x_prompt.shape
    w = dict(
        depth=depth, n_a=n_a, d_pool=c, hd_fox=hd, mem_heads=heads,
        norm_g=norm_g, g_kv=g_kv, w_f=w_f, b_f=b_f, pool_scale=pool_scale,
        wg=[[_pad_ff(w_ffn_gate, l, s, 1) for s in range(2)] for l in range(depth)],
        wu=[[_pad_ff(w_ffn_up, l, s, 1) for s in range(2)] for l in range(depth)],
        wd=[[_pad_ff(w_ffn_down, l, s, 0) for s in range(2)] for l in range(depth)],
        w_in_a=w_in_a.astype(BF16), w_out_a=w_out_a.astype(BF16),
        w_in_b=w_in_b.astype(BF16), w_out_b=w_out_b.astype(BF16),
        pool_w=pool_w.astype(BF16), w_kv=w_kv.astype(BF16),
    )
    mem2 = mem_prompt.reshape(bp * n_mem, d)
    mem_kv = [mm(rms_cast(mem2, g_mem[l]), w_mem_kv[l].astype(BF16), [F32])[0] for l in range(depth)]
    mem_k_p = jnp.stack([kv[:, :dq].reshape(bp, n_mem, dq) for kv in mem_kv])
    mem_v_p = jnp.stack([kv[:, dq:].reshape(bp, n_mem, dq) for kv in mem_kv])

    zero_prev = jnp.zeros((n_a, bp, HALO, c), F32)
    y_p, pool_p, k_p, v_p, logf_p = _trunk(x_prompt, zero_prev, mem_k_p, mem_v_p, None, w)

    prev_s = jnp.pad(state_pool.astype(F32), ((0, 0), (0, 0), (1, 0), (0, 0)))
    past = (cache_fox_k, cache_fox_v, cache_fox_logf)
    y_s, pool_s, k_s, v_s, logf_s = _trunk(
        x_sample, prev_s, cache_mem_k.reshape(depth, db, n_mem, dq),
        cache_mem_v.reshape(depth, db, n_mem, dq), past, w)

    return (y_p, y_s, k_p, v_p, logf_p,
            mem_k_p.reshape(depth, bp, n_mem, heads, mem_hd),
            mem_v_p.reshape(depth, bp, n_mem, heads, mem_hd),
            pool_p, k_s, v_s, logf_s, pool_s)
```

```python
import functools

import jax
import jax.numpy as jnp
from jax import lax
from jax.experimental import pallas as pl
from jax.experimental.pallas import tpu as pltpu

EPS = 1e-6
POOL_WINDOWS = (2, 4, 8, 16)
POOL_STATE = max(POOL_WINDOWS) - 1
HALO = POOL_STATE + 1
LANE = 128
FF_ALIGN = 1024
MASK_VALUE = -1e30
VMEM_CAP = 60 * 1024 * 1024
F32 = jnp.float32
BF16 = jnp.bfloat16


def _tile(n, pref):
    if n <= pref:
        return n
    t = pref
    while n % t:
        t //= 2
    return t


def _params(sem, est_bytes):
    limit = int(min(VMEM_CAP, max(32 * 1024 * 1024, est_bytes * 5 // 4 + (4 << 20))))
    return pltpu.CompilerParams(dimension_semantics=sem, vmem_limit_bytes=limit)


def _rms_normalize(xf):
    return xf * lax.rsqrt(jnp.mean(xf * xf, axis=-1, keepdims=True) + EPS)


def _rms_cast_kernel(x_ref, g_ref, o_ref):
    o_ref[...] = (_rms_normalize(x_ref[...]) * g_ref[...]).astype(o_ref.dtype)


def rms_cast(x, g):
    m, d = x.shape
    tm = _tile(m, 512)
    return pl.pallas_call(
        _rms_cast_kernel,
        out_shape=jax.ShapeDtypeStruct((m, d), BF16),
        grid=(m // tm,),
        in_specs=[pl.BlockSpec((tm, d), lambda i: (i, 0)),
                  pl.BlockSpec((1, d), lambda i: (0, 0))],
        out_specs=pl.BlockSpec((tm, d), lambda i: (i, 0)),
        compiler_params=_params(("parallel",), 2 * tm * d * 6),
        name="rms_cast",
    )(x, g.reshape(1, d).astype(F32))


def _mm_kernel(h_ref, w_ref, *o_refs, scale):
    acc = jnp.dot(h_ref[...], w_ref[...], preferred_element_type=F32)
    if scale is not None:
        acc = acc * scale
    for o_ref in o_refs:
        o_ref[...] = acc.astype(o_ref.dtype)


def mm(h, w, out_dtypes, scale=None, cols=None):
    m, k = h.shape
    c0, c1 = cols if cols is not None else (0, w.shape[1])
    n = c1 - c0
    tm, tn = _tile(m, 1024), _tile(n, 1024)
    assert c0 % tn == 0
    j0 = c0 // tn
    est = 2 * (tm * k * 2 + k * tn * 2) + tm * tn * 4 + sum(
        2 * tm * tn * jnp.dtype(t).itemsize for t in out_dtypes)
    outs = pl.pallas_call(
        functools.partial(_mm_kernel, scale=scale),
        out_shape=[jax.ShapeDtypeStruct((m, n), t) for t in out_dtypes],
        grid=(m // tm, n // tn),
        in_specs=[pl.BlockSpec((tm, k), lambda i, j: (i, 0)),
                  pl.BlockSpec((k, tn), lambda i, j: (0, j + j0))],
        out_specs=[pl.BlockSpec((tm, tn), lambda i, j: (i, j)) for _ in out_dtypes],
        compiler_params=_params(("parallel", "parallel"), est),
        name="mm",
    )(h, w)
    return outs


def _ffn_up_kernel(h_ref, wg_ref, wu_ref, o_ref):
    h = h_ref[...]
    g = jnp.dot(h, wg_ref[...], preferred_element_type=F32)
    u = jnp.dot(h, wu_ref[...], preferred_element_type=F32)
    o_ref[...] = (g * jax.nn.sigmoid(g) * u).astype(o_ref.dtype)


def ffn_up(h, wg, wu):
    m, k = h.shape
    n = wg.shape[1]
    tm, tn = _tile(m, 1024), _tile(n, 512)
    est = 2 * (tm * k * 2 + 2 * k * tn * 2 + tm * tn * 2) + 3 * tm * tn * 4
    return pl.pallas_call(
        _ffn_up_kernel,
        out_shape=jax.ShapeDtypeStruct((m, n), BF16),
        grid=(m // tm, n // tn),
        in_specs=[pl.BlockSpec((tm, k), lambda i, j: (i, 0)),
                  pl.BlockSpec((k, tn), lambda i, j: (0, j)),
                  pl.BlockSpec((k, tn), lambda i, j: (0, j))],
        out_specs=pl.BlockSpec((tm, tn), lambda i, j: (i, j)),
        compiler_params=_params(("parallel", "parallel"), est),
        name="ffn_up",
    )(h, wg, wu)


def _mm_resid_kernel(*refs, kblocks, coef, n_next, ntiles, nchunk):
    n_lhs = len(kblocks)
    a_refs = refs[:n_lhs]
    w_ref, x_ref, gp_ref = refs[n_lhs:n_lhs + 3]
    gn_refs = refs[n_lhs + 3:n_lhs + 3 + n_next]
    xo_ref = refs[n_lhs + 3 + n_next]
    h_refs = refs[n_lhs + 4 + n_next:-2]
    acc_refs = refs[-2:]
    i, k = pl.program_id(0), pl.program_id(1)
    rows = xo_ref.shape[0]

    @pl.when((i == 0) & (k == 0))
    def _():
        for acc_ref in acc_refs:
            acc_ref[...] = jnp.zeros_like(acc_ref)

    def accumulate(acc_ref):
        a = a_refs[0][...]
        start = kblocks[0]
        for a_ref, nb in zip(a_refs[1:], kblocks[1:]):
            a = jnp.where(k >= start, a_ref[...], a)
            start += nb
        prev = jnp.where(k == 0, 0.0, acc_ref[...])
        acc_ref[...] = prev + jnp.dot(a, w_ref[...], preferred_element_type=F32)

    def epilogue(acc_ref):
        r0 = pl.multiple_of(jnp.minimum(k, nchunk - 1) * rows, rows)
        y = acc_ref[pl.ds(r0, rows), :]
        xn = x_ref[...] + coef * (_rms_normalize(y) * gp_ref[...])
        xo_ref[...] = xn
        if n_next:
            r = _rms_normalize(xn)
            for gn_ref, h_ref in zip(gn_refs, h_refs):
                h_ref[...] = (r * gn_ref[...]).astype(h_ref.dtype)

    for parity in range(2):
        cur, prev = acc_refs[parity], acc_refs[1 - parity]

        @pl.when((i % 2 == parity) & (i < ntiles))
        def _(cur=cur, prev=prev):
            accumulate(cur)
            epilogue(prev)

    @pl.when(i == ntiles)
    def _():
        epilogue(acc_refs[1 - ntiles % 2])


def mm_resid(lhs, w, x, g_post, coef, g_next):
    m, d = x.shape
    widths = [a.shape[1] for a in lhs]
    assert sum(widths) == w.shape[0] and w.shape[1] == d
    tk = 1024
    while any(kw % tk for kw in widths):
        tk //= 2
    tm = _tile(m, 512)
    ntiles = m // tm
    kblocks = tuple(kw // tk for kw in widths)
    starts = [sum(kblocks[:j]) for j in range(len(lhs))]
    n_next = len(g_next)
    nk = sum(kblocks)
    nchunk = 1
    while nchunk * 2 <= min(nk, 8) and tm % (nchunk * 2 * 16) == 0:
        nchunk *= 2
    rows = tm // nchunk
    gain = pl.BlockSpec((1, d), lambda i, k: (0, 0))

    def lhs_spec(start, nb):
        return pl.BlockSpec(
            (tm, tk), lambda i, k: (jnp.minimum(i, ntiles - 1), jnp.clip(k - start, 0, nb - 1)))

    chunk = pl.BlockSpec(
        (rows, d), lambda i, k: (jnp.maximum((i - 1) * nchunk + jnp.minimum(k, nchunk - 1), 0), 0))
    est = len(lhs) * 2 * tm * tk * 2 + 2 * tk * d * 2 + 2 * tm * d * 4 \
        + 2 * rows * d * (8 + 2 * n_next) + 8 * rows * d * 4 + tm * tk * 2
    outs = pl.pallas_call(
        functools.partial(_mm_resid_kernel, kblocks=kblocks, coef=coef, n_next=n_next,
                          ntiles=ntiles, nchunk=nchunk),
        out_shape=[jax.ShapeDtypeStruct((m, d), F32)]
        + [jax.ShapeDtypeStruct((m, d), BF16) for _ in range(n_next)],
        grid=(ntiles + 1, nk),
        in_specs=[lhs_spec(s, nb) for s, nb in zip(starts, kblocks)]
        + [pl.BlockSpec((tk, d), lambda i, k: (k, 0)), chunk, gain] + [gain] * n_next,
        out_specs=[chunk] * (1 + n_next),
        scratch_shapes=[pltpu.VMEM((tm, d), F32), pltpu.VMEM((tm, d), F32)],
        compiler_params=_params(("arbitrary", "arbitrary"), est),
        name="mm_resid",
    )(*lhs, w, x, g_post.reshape(1, d).astype(F32), *[g.reshape(1, d).astype(F32) for g in g_next])
    return outs[0], outs[1:]


def _pool_kernel(u_ref, halo_ref, prev_ref, w_ref, sc_ref, o_ref, ext_ref, *, tm, pos0, group):
    i = pl.program_id(1)

    @pl.when(i == 0)
    def _():
        ext_ref[0:HALO, :] = prev_ref[0]

    @pl.when(i > 0)
    def _():
        ext_ref[0:HALO, :] = halo_ref[0]

    ext_ref[HALO:HALO + tm, :] = u_ref[0]
    pos = pos0 + i * tm + lax.broadcasted_iota(jnp.int32, (tm, 1), 0)
    for gi, win in enumerate(POOL_WINDOWS):
        sl = slice(gi * group, (gi + 1) * group)
        ug = ext_ref[HALO:HALO + tm, sl]
        wsum = ug
        for j in range(1, win):
            wsum = wsum + ext_ref[HALO - j:HALO - j + tm, sl]
        cnt = jnp.minimum(pos + 1, win).astype(F32)
        diff = wsum / cnt - ug
        y = jnp.dot(diff.astype(BF16), w_ref[gi], preferred_element_type=F32)
        o_ref[0, :, sl] = (y * sc_ref[:, sl]).astype(o_ref.dtype)


def pool_mixer(u, prev, pos0, pool_w, pool_scale):
    b, n, c = u.shape
    group = c // len(POOL_WINDOWS)
    tm = _tile(n, 256)
    steps = tm // HALO
    est = 2 * (tm * c * 4 + 2 * HALO * c * 4 + pool_w.size * 2 + tm * c * 2) + (tm + HALO) * c * 4 \
        + 4 * tm * group * 4
    return pl.pallas_call(
        functools.partial(_pool_kernel, tm=tm, pos0=pos0, group=group),
        out_shape=jax.ShapeDtypeStruct((b, n, c), BF16),
        grid=(b, n // tm),
        in_specs=[pl.BlockSpec((1, tm, c), lambda bb, i: (bb, i, 0)),
                  pl.BlockSpec((1, HALO, c), lambda bb, i: (bb, jnp.maximum(i * steps - 1, 0), 0)),
                  pl.BlockSpec((1, HALO, c), lambda bb, i: (bb, 0, 0)),
                  pl.BlockSpec(pool_w.shape, lambda bb, i: (0, 0, 0)),
                  pl.BlockSpec((1, c), lambda bb, i: (0, 0))],
        out_specs=pl.BlockSpec((1, tm, c), lambda bb, i: (bb, i, 0)),
        scratch_shapes=[pltpu.VMEM((tm + HALO, c), F32)],
        compiler_params=_params(("parallel", "parallel"), est),
        name="pool_mixer",
    )(u, u, prev, pool_w, pool_scale.reshape(1, c).astype(F32))


def _mem_attn_kernel(q_ref, k_ref, v_ref, o_ref, *, heads, hd):
    scale = hd ** -0.5
    for hh in range(heads):
        sl = slice(hh * hd, (hh + 1) * hd)
        q = q_ref[0, :, sl]
        k = k_ref[0, :, sl].astype(BF16)
        v = v_ref[0, :, sl].astype(BF16)
        s = lax.dot_general(q, k, (((1,), (1,)), ((), ())), preferred_element_type=F32) * scale
        p = jnp.exp(s - jnp.max(s, axis=-1, keepdims=True))
        p = p / jnp.sum(p, axis=-1, keepdims=True)
        o_ref[0, :, sl] = jnp.dot(p.astype(BF16), v, preferred_element_type=F32).astype(o_ref.dtype)


def mem_attend(qm, mk, mv, heads):
    b, n, dq = qm.shape
    nm = mk.shape[1]
    tm = _tile(n, 512)
    est = 2 * (2 * tm * dq * 2 + 2 * nm * dq * 4) + 4 * tm * nm * 4
    return pl.pallas_call(
        functools.partial(_mem_attn_kernel, heads=heads, hd=dq // heads),
        out_shape=jax.ShapeDtypeStruct((b, n, dq), BF16),
        grid=(b, n // tm),
        in_specs=[pl.BlockSpec((1, tm, dq), lambda bb, i: (bb, i, 0)),
                  pl.BlockSpec((1, nm, dq), lambda bb, i: (bb, 0, 0)),
                  pl.BlockSpec((1, nm, dq), lambda bb, i: (bb, 0, 0))],
        out_specs=pl.BlockSpec((1, tm, dq), lambda bb, i: (bb, i, 0)),
        compiler_params=_params(("parallel", "parallel"), est),
        name="mem_attend",
    )(qm, mk, mv)


def _logf_kernel(h_ref, w_ref, b_ref, o_ref):
    z = jnp.dot(h_ref[...], w_ref[...], preferred_element_type=F32) + b_ref[...]
    o_ref[...] = jnp.minimum(z, 0.0) - jnp.log1p(jnp.exp(-jnp.abs(z)))


def log_forget(hk, w_f, b_f):
    m, k = hk.shape
    nh = w_f.shape[1]
    wp = jnp.pad(w_f, ((0, 0), (0, LANE - nh))).astype(BF16)
    bp = jnp.pad(b_f, (0, LANE - nh)).reshape(1, LANE).astype(F32)
    tm = _tile(m, 1024)
    out = pl.pallas_call(
        _logf_kernel,
        out_shape=jax.ShapeDtypeStruct((m, LANE), F32),
        grid=(m // tm,),
        in_specs=[pl.BlockSpec((tm, k), lambda i: (i, 0)),
                  pl.BlockSpec((k, LANE), lambda i: (0, 0)),
                  pl.BlockSpec((1, LANE), lambda i: (0, 0))],
        out_specs=pl.BlockSpec((tm, LANE), lambda i: (i, 0)),
        compiler_params=_params(("parallel",), 2 * (tm * k * 2 + k * LANE * 2 + tm * LANE * 4)),
        name="log_forget",
    )(hk, wp, bp)
    return out[:, :nh]


def _cumsum_kernel(x_ref, o_ref, carry_ref, *, tb):
    @pl.when(pl.program_id(1) == 0)
    def _():
        carry_ref[...] = jnp.zeros_like(carry_ref)

    x = x_ref[0]
    row = lax.broadcasted_iota(jnp.int32, x.shape, 0)
    d = 1
    while d < tb:
        x = x + jnp.where(row >= d, pltpu.roll(x, d, 0), 0.0)
        d *= 2
    x = x + carry_ref[0:1, :]
    o_ref[0] = x
    carry_ref[...] = jnp.broadcast_to(x[tb - 1:tb, :], carry_ref.shape)


def cumsum_time(logf):
    b, t, nh = logf.shape
    tb = 256
    tp = -(-t // tb) * tb
    x = jnp.pad(logf, ((0, 0), (0, tp - t), (0, LANE - nh)))
    out = pl.pallas_call(
        functools.partial(_cumsum_kernel, tb=tb),
        out_shape=jax.ShapeDtypeStruct((b, tp, LANE), F32),
        grid=(b, tp // tb),
        in_specs=[pl.BlockSpec((1, tb, LANE), lambda bb, i: (bb, i, 0))],
        out_specs=pl.BlockSpec((1, tb, LANE), lambda bb, i: (bb, i, 0)),
        scratch_shapes=[pltpu.VMEM((8, LANE), F32)],
        compiler_params=_params(("parallel", "arbitrary"), 8 * tb * LANE * 4),
        name="cumsum_time",
    )(x)
    return out[:, :t, :nh]


LOG2E = 1.4426950408889634
_NT = (((1,), (1,)), ((), ()))


def _lanes(x, reps):
    return x if reps == 1 else jnp.concatenate([x] * reps, axis=1)


def _fox_prompt_kernel(q_ref, k_ref, v_ref, f_ref, o_ref, s0_ref, s1_ref, m_ref, acc_ref,
                       *, tq, tk, hd, nhb):
    qi = pl.program_id(1)
    m_ref[...] = jnp.full_like(m_ref, MASK_VALUE)
    acc_ref[...] = jnp.zeros_like(acc_ref)
    ones = jnp.ones((tk, hd), BF16)

    def logits(kb, s_ref):
        off = pl.multiple_of(kb * tk, tk)
        for hh in range(nhb):
            sl = slice(hh * hd, (hh + 1) * hd)
            s = lax.dot_general(q_ref[:, sl], k_ref[pl.ds(off, tk), sl], _NT,
                                preferred_element_type=F32)
            s_ref[hh] = s - f_ref[hh, kb] * LOG2E

    def attend(kb, s_ref, diag=None):
        off = pl.multiple_of(kb * tk, tk)
        for hh in range(nhb):
            sl = slice(hh * hd, (hh + 1) * hd)
            s = s_ref[hh]
            if diag is not None:
                row = lax.broadcasted_iota(jnp.int32, s.shape, 0)
                col = lax.broadcasted_iota(jnp.int32, s.shape, 1)
                s = jnp.where(col + diag * tk <= row, s, MASK_VALUE)
            m_prev = m_ref[hh]
            m_new = jnp.maximum(m_prev, jnp.max(s, axis=-1, keepdims=True))
            alpha = jnp.exp2(m_prev - m_new)
            p = jnp.exp2(s - _lanes(m_new, tk // LANE))
            v = v_ref[pl.ds(off, tk), sl]
            pv = jnp.dot(p.astype(BF16), jnp.concatenate([v, ones], axis=1),
                         preferred_element_type=F32)
            acc_ref[hh] = _lanes(alpha, 2 * hd // LANE) * acc_ref[hh] + pv
            m_ref[hh] = m_new

    def body(jj, carry):
        kb = 2 * jj
        logits(kb + 1, s1_ref)
        attend(kb, s0_ref)
        logits(kb + 2, s0_ref)
        attend(kb + 1, s1_ref)
        return carry

    logits(0, s0_ref)
    lax.fori_loop(0, qi, body, 0)
    logits(2 * qi + 1, s1_ref)
    attend(2 * qi, s0_ref, diag=0)
    attend(2 * qi + 1, s1_ref, diag=1)

    for hh in range(nhb):
        acc = acc_ref[hh]
        o_ref[:, hh * hd:(hh + 1) * hd] = (acc[:, :hd] / acc[:, hd:]).astype(o_ref.dtype)


def fox_prompt(q, k, v, f_cum, hd):
    n, dm = q.shape
    nh = dm // hd
    nhb = 2 if nh % 2 == 0 else 1
    tq = _tile(n, 1024)
    tk = tq // 2
    assert hd == LANE and tk % LANE == 0
    f_t = f_cum.T.reshape(nh, n // tk, 1, tk)
    wb = nhb * hd
    est = 2 * (2 * tq * wb * 2 + 2 * n * wb * 2 + nhb * (n // tk) * 8 * tk * 4) \
        + nhb * (tq * LANE * 4 + tq * 2 * hd * 4 + 5 * tq * tk * 4)
    return pl.pallas_call(
        functools.partial(_fox_prompt_kernel, tq=tq, tk=tk, hd=hd, nhb=nhb),
        out_shape=jax.ShapeDtypeStruct((n, dm), BF16),
        grid=(nh // nhb, n // tq),
        in_specs=[pl.BlockSpec((tq, wb), lambda h, i: (i, h)),
                  pl.BlockSpec((n, wb), lambda h, i: (0, h)),
                  pl.BlockSpec((n, wb), lambda h, i: (0, h)),
                  pl.BlockSpec((nhb, n // tk, 1, tk), lambda h, i: (h, 0, 0, 0))],
        out_specs=pl.BlockSpec((tq, wb), lambda h, i: (i, h)),
        scratch_shapes=[pltpu.VMEM((nhb, tq, tk), F32), pltpu.VMEM((nhb, tq, tk), F32),
                        pltpu.VMEM((nhb, tq, LANE), F32), pltpu.VMEM((nhb, tq, 2 * hd), F32)],
        compiler_params=_params(("parallel", "parallel"), est),
        name="fox_prompt",
    )(q, k, v, f_t)


def _fox_cached_kernel(q_ref, kp_ref, vp_ref, kn_ref, vn_ref, fp_ref, fn_ref, rh_ref, rt_ref,
                       ch_ref, ct_ref, o_ref, m_ref, acc_ref, *, sub, hd):
    j = pl.program_id(1)
    q = q_ref[0]
    rh = rh_ref[...]

    @pl.when(j == 0)
    def _():
        m_ref[...] = jnp.full_like(m_ref, MASK_VALUE)
        acc_ref[...] = jnp.zeros_like(acc_ref)

    def logits(k, f, valid):
        s = lax.dot_general(q, k, _NT, preferred_element_type=F32) - f * LOG2E
        return jnp.where(valid, s, MASK_VALUE)

    def absorb(s, v):
        width = s.shape[1]
        m_prev = m_ref[...]
        m_new = jnp.maximum(m_prev, jnp.max(s, axis=-1, keepdims=True))
        alpha = jnp.exp2(m_prev - m_new)
        p = jnp.exp2(s - (_lanes(m_new, width // LANE) if width % LANE == 0 else m_new[:, :1]))
        vaug = jnp.concatenate([v, jnp.ones_like(v)], axis=1)
        pv = jnp.dot(p.astype(BF16), vaug, preferred_element_type=F32)
        acc_ref[...] = _lanes(alpha, 2 * hd // LANE) * acc_ref[...] + pv
        m_ref[...] = m_new

    subs = [slice(c0, c0 + sub) for c0 in range(0, kp_ref.shape[1], sub)]
    s_all = [logits(kp_ref[0, cs, :].astype(BF16), fp_ref[0, :, cs], rh == ch_ref[:, cs])
             for cs in subs]
    for s, cs in zip(s_all, subs):
        absorb(s, vp_ref[0, cs, :].astype(BF16))

    @pl.when(j == pl.num_programs(1) - 1)
    def _():
        r = kn_ref.shape[1]
        absorb(logits(kn_ref[0], fn_ref[0], (rh == ch_ref[:, :r]) & (ct_ref[...] <= rt_ref[...])),
               vn_ref[0])
        acc = acc_ref[...]
        o_ref[0] = (acc[:, :hd] / acc[:, hd:]).astype(o_ref.dtype)


def fox_cached(q, k_past, v_past, k_new, v_new, f_all, hd):
    b, n, dm = q.shape
    p, nh = k_past.shape[1:3]
    r = n * nh
    tp = _tile(p, 256)
    c = tp * nh
    sub = c // 2 if c % (2 * LANE * nh) == 0 else c
    assert c % LANE == 0 and c >= r
    heads = jnp.arange(c, dtype=jnp.int32) % nh
    times = jnp.arange(r, dtype=jnp.int32) // nh
    new = pl.BlockSpec((1, r, hd), lambda bb, j: (bb, 0, 0))
    past = pl.BlockSpec((1, c, hd), lambda bb, j: (bb, j, 0))
    whole = lambda shape: pl.BlockSpec(shape, lambda bb, j: (0, 0))
    est = 2 * (2 * c * hd * 4 + 4 * r * hd * 2) + 6 * r * sub * 4 + 3 * r * 2 * hd * 4
    out = pl.pallas_call(
        functools.partial(_fox_cached_kernel, sub=sub, hd=hd),
        out_shape=jax.ShapeDtypeStruct((b, r, hd), BF16),
        grid=(b, p // tp),
        in_specs=[new, past, past, new, new,
                  pl.BlockSpec((1, 1, c), lambda bb, j: (bb, 0, j)),
                  pl.BlockSpec((1, 1, r), lambda bb, j: (bb, 0, 0)),
                  whole((r, 1)), whole((r, 1)), whole((1, c)), whole((1, r))],
        out_specs=new,
        scratch_shapes=[pltpu.VMEM((r, LANE), F32), pltpu.VMEM((r, 2 * hd), F32)],
        compiler_params=_params(("parallel", "arbitrary"), est),
        name="fox_cached",
    )(q.reshape(b, r, hd), k_past.reshape(b, p * nh, hd), v_past.reshape(b, p * nh, hd),
      k_new.reshape(b, r, hd), v_new.reshape(b, r, hd),
      f_all[:, :p].reshape(b, 1, p * nh), f_all[:, p:].reshape(b, 1, r),
      heads[:r].reshape(r, 1), times.reshape(r, 1), heads.reshape(1, c), times.reshape(1, r))
    return out.reshape(b, n, dm)


def _trunk(x, pool_prev, mem_k, mem_v, past, w):
    b, n, d = x.shape
    m = b * n
    depth, n_a = w["depth"], w["n_a"]
    c, hd, heads = w["d_pool"], w["hd_fox"], w["mem_heads"]
    assert n >= POOL_STATE
    pos0 = 0 if past is None else past[0].shape[1]
    x2 = x.reshape(m, d)
    h = rms_cast(x2, w["norm_g"][0, 0])
    new_pool = []
    k_new = v_new = logf_new = None
    k_bf = v_bf = f_all = None
    for l in range(depth):
        g = w["norm_g"][l]
        a = ffn_up(h, w["wg"][l][0], w["wu"][l][0])
        x2, (h,) = mm_resid([a], w["wd"][l][0], x2, g[1], 0.5, [g[2]])
        if l < n_a:
            w_in, w_out = w["w_in_a"][l], w["w_out_a"][l]
            (u,) = mm(h, w_in, [F32], cols=(0, c))
            u3 = u.reshape(b, n, c)
            main = pool_mixer(u3, pool_prev[l], pos0, w["pool_w"][l], w["pool_scale"][l])
            new_pool.append(u3[:, n - POOL_STATE:, :])
        else:
            j = l - n_a
            w_in, w_out = w["w_in_b"][j], w["w_out_b"][j]
            (q,) = mm(h, w_in, [BF16], scale=LOG2E * hd ** -0.5, cols=(0, c))
            if past is None:
                main = fox_prompt(q, k_bf, v_bf, f_all[0], hd).reshape(b, n, c)
            else:
                main = fox_cached(q.reshape(b, n, c), past[0], past[1], k_bf.reshape(b, n, c),
                                  v_bf.reshape(b, n, c), f_all, hd)
        (qm,) = mm(h, w_in, [BF16], cols=(c, d))
        mem = mem_attend(qm.reshape(b, n, d - c), mem_k[l], mem_v[l], heads)
        x2, (h,) = mm_resid([main.reshape(m, c), mem.reshape(m, d - c)], w_out, x2, g[3], 1.0, [g[4]])
        a = ffn_up(h, w["wg"][l][1], w["wu"][l][1])
        g_next = []
        if l + 1 < depth:
            g_next.append(w["norm_g"][l + 1, 0])
        if l == n_a - 1:
            g_next.append(w["g_kv"])
        x2, hs = mm_resid([a], w["wd"][l][1], x2, g[5], 0.5, g_next)
        if l + 1 < depth:
            h = hs[0]
        if l == n_a - 1:
            hk = hs[-1]
            k_new, k_bf = mm(hk, w["w_kv"], [F32, BF16], cols=(0, c))
            v_new, v_bf = mm(hk, w["w_kv"], [F32, BF16], cols=(c, 2 * c))
            logf_new = log_forget(hk, w["w_f"], w["b_f"]).reshape(b, n, -1)
            logf_all = logf_new if past is None else jnp.concatenate(
                [past[2].astype(F32), logf_new], axis=1)
            f_all = cumsum_time(logf_all)
    nh = c // hd
    return (x2.reshape(b, n, d), jnp.stack(new_pool), k_new.reshape(b, n, nh, hd),
            v_new.reshape(b, n, nh, hd), logf_new)


def _cast_pad_kernel(x_ref, o_ref, *, rows, cols):
    tr, tc = o_ref.shape
    r = pl.program_id(0) * tr + lax.broadcasted_iota(jnp.int32, (tr, tc), 0)
    c = pl.program_id(1) * tc + lax.broadcasted_iota(jnp.int32, (tr, tc), 1)
    o_ref[...] = jnp.where((r < rows) & (c < cols), x_ref[...], 0.0).astype(o_ref.dtype)


def _pad_ff(w4, l, s, axis):
    rows, cols = w4.shape[2:]
    shape = [rows, cols]
    shape[axis] += -shape[axis] % FF_ALIGN
    tr, tc = _tile(shape[0], 1024), _tile(shape[1], 1024)
    return pl.pallas_call(
        functools.partial(_cast_pad_kernel, rows=rows, cols=cols),
        out_shape=jax.ShapeDtypeStruct(tuple(shape), BF16),
        grid=(shape[0] // tr, shape[1] // tc),
        in_specs=[pl.BlockSpec((None, None, tr, tc), lambda i, j: (l, s, i, j))],
        out_specs=pl.BlockSpec((tr, tc), lambda i, j: (i, j)),
        compiler_params=_params(("parallel", "parallel"), 2 * tr * tc * 6 + 3 * tr * tc * 4),
        name="cast_pad",
    )(w4)


def kernel(x_prompt, x_sample, mem_prompt, cache_fox_k, cache_fox_v, cache_fox_logf, cache_mem_k, cache_mem_v, state_pool, norm_g, w_ffn_gate, w_ffn_up, w_ffn_down, w_in_a, pool_w, pool_scale, w_out_a, w_in_b, w_out_b, g_kv, w_kv, w_f, b_f, g_mem, w_mem_kv):
    depth = norm_g.shape[0]
    n_a, db, _, c = state_pool.shape
    _, p, nh, hd = cache_fox_k.shape
    _, _, n_mem, heads, mem_hd = cache_mem_k.shape
    dq = heads * mem_hd
    bp, _, d = x_prompt.shape
    w = dict(
        depth=depth, n_a=n_a, d_pool=c, hd_fox=hd, mem_heads=heads,
        norm_g=norm_g, g_kv=g_kv, w_f=w_f, b_f=b_f, pool_scale=pool_scale,
        wg=[[_pad_ff(w_ffn_gate, l, s, 1) for s in range(2)] for l in range(depth)],
        wu=[[_pad_ff(w_ffn_up, l, s, 1) for s in range(2)] for l in range(depth)],
        wd=[[_pad_ff(w_ffn_down, l, s, 0) for s in range(2)] for l in range(depth)],
        w_in_a=w_in_a.astype(BF16), w_out_a=w_out_a.astype(BF16),
        w_in_b=w_in_b.astype(BF16), w_out_b=w_out_b.astype(BF16),
        pool_w=pool_w.astype(BF16), w_kv=w_kv.astype(BF16),
    )
    mem2 = mem_prompt.reshape(bp * n_mem, d)
    mem_kv = [mm(rms_cast(mem2, g_mem[l]), w_mem_kv[l].astype(BF16), [F32])[0] for l in range(depth)]
    mem_k_p = jnp.stack([kv[:, :dq].reshape(bp, n_mem, dq) for kv in mem_kv])
    mem_v_p = jnp.stack([kv[:, dq:].reshape(bp, n_mem, dq) for kv in mem_kv])

    zero_prev = jnp.zeros((n_a, bp, HALO, c), F32)
    y_p, pool_p, k_p, v_p, logf_p = _trunk(x_prompt, zero_prev, mem_k_p, mem_v_p, None, w)

    prev_s = jnp.pad(state_pool.astype(F32), ((0, 0), (0, 0), (1, 0), (0, 0)))
    past = (cache_fox_k, cache_fox_v, cache_fox_logf)
    y_s, pool_s, k_s, v_s, logf_s = _trunk(
        x_sample, prev_s, cache_mem_k.reshape(depth, db, n_mem, dq),
        cache_mem_v.reshape(depth, db, n_mem, dq), past, w)

    return (y_p, y_s, k_p, v_p, logf_p,
            mem_k_p.reshape(depth, bp, n_mem, heads, mem_hd),
            mem_v_p.reshape(depth, bp, n_mem, heads, mem_hd),
            pool_p, k_s, v_s, logf_s, pool_s)
```

```python
import functools

import jax
import jax.numpy as jnp
from jax import lax
from jax.experimental import pallas as pl
from jax.experimental.pallas import tpu as pltpu

EPS = 1e-6
POOL_WINDOWS = (2, 4, 8, 16)
POOL_STATE = max(POOL_WINDOWS) - 1
HALO = POOL_STATE + 1
LANE = 128
BF16_ROWS = 16
MASK_VALUE = -1e30
VMEM_CAP = 60 * 1024 * 1024
VMEM_FLOOR = 32 * 1024 * 1024
VMEM_SLACK = 4 * 1024 * 1024
F32 = jnp.float32
BF16 = jnp.bfloat16

MM_TILE = 1024
FFN_UP_TILE_N = 512
FF_ALIGN = 1024
RESID_TILE_M, RESID_TILE_K = 512, 1024
RESID_MAX_CHUNKS = 8
ROW_TILE = 512
POOL_TILE = 256
FOX_TILE_Q = 1024
FOX_PAST_TILE = 256
CUMSUM_TILE = 256


def _tile(n, pref):
    if n <= pref:
        return n
    t = pref
    while n % t:
        t //= 2
    return t


def _params(sem, est_bytes):
    limit = int(min(VMEM_CAP, max(VMEM_FLOOR, est_bytes * 5 // 4 + VMEM_SLACK)))
    return pltpu.CompilerParams(dimension_semantics=sem, vmem_limit_bytes=limit)


def _rms_normalize(xf):
    return xf * lax.rsqrt(jnp.mean(xf * xf, axis=-1, keepdims=True) + EPS)


def _rms_cast_kernel(x_ref, g_ref, o_ref):
    o_ref[...] = (_rms_normalize(x_ref[...]) * g_ref[...]).astype(o_ref.dtype)


def rms_cast(x, g):
    m, d = x.shape
    tm = _tile(m, ROW_TILE)
    return pl.pallas_call(
        _rms_cast_kernel,
        out_shape=jax.ShapeDtypeStruct((m, d), BF16),
        grid=(m // tm,),
        in_specs=[pl.BlockSpec((tm, d), lambda i: (i, 0)),
                  pl.BlockSpec((1, d), lambda i: (0, 0))],
        out_specs=pl.BlockSpec((tm, d), lambda i: (i, 0)),
        compiler_params=_params(("parallel",), 2 * tm * d * 6),
        name="rms_cast",
    )(x, g.reshape(1, d).astype(F32))


def _mm_kernel(h_ref, w_ref, *o_refs, scale):
    acc = jnp.dot(h_ref[...], w_ref[...], preferred_element_type=F32)
    if scale is not None:
        acc = acc * scale
    for o_ref in o_refs:
        o_ref[...] = acc.astype(o_ref.dtype)


def mm(h, w, out_dtypes, scale=None, cols=None):
    m, k = h.shape
    c0, c1 = cols if cols is not None else (0, w.shape[1])
    n = c1 - c0
    tm, tn = _tile(m, MM_TILE), _tile(n, MM_TILE)
    assert c0 % tn == 0
    j0 = c0 // tn
    est = 2 * (tm * k * 2 + k * tn * 2) + tm * tn * 4 + sum(
        2 * tm * tn * jnp.dtype(t).itemsize for t in out_dtypes)
    outs = pl.pallas_call(
        functools.partial(_mm_kernel, scale=scale),
        out_shape=[jax.ShapeDtypeStruct((m, n), t) for t in out_dtypes],
        grid=(m // tm, n // tn),
        in_specs=[pl.BlockSpec((tm, k), lambda i, j: (i, 0)),
                  pl.BlockSpec((k, tn), lambda i, j: (0, j + j0))],
        out_specs=[pl.BlockSpec((tm, tn), lambda i, j: (i, j)) for _ in out_dtypes],
        compiler_params=_params(("parallel", "parallel"), est),
        name="mm",
    )(h, w)
    return outs


def _ffn_up_kernel(h_ref, wg_ref, wu_ref, o_ref):
    h = h_ref[...]
    g = jnp.dot(h, wg_ref[...], preferred_element_type=F32)
    u = jnp.dot(h, wu_ref[...], preferred_element_type=F32)
    o_ref[...] = (g * jax.nn.sigmoid(g) * u).astype(o_ref.dtype)


def ffn_up(h, wg, wu):
    m, k = h.shape
    n = wg.shape[1]
    tm, tn = _tile(m, MM_TILE), _tile(n, FFN_UP_TILE_N)
    est = 2 * (tm * k * 2 + 2 * k * tn * 2 + tm * tn * 2) + 3 * tm * tn * 4
    return pl.pallas_call(
        _ffn_up_kernel,
        out_shape=jax.ShapeDtypeStruct((m, n), BF16),
        grid=(m // tm, n // tn),
        in_specs=[pl.BlockSpec((tm, k), lambda i, j: (i, 0)),
                  pl.BlockSpec((k, tn), lambda i, j: (0, j)),
                  pl.BlockSpec((k, tn), lambda i, j: (0, j))],
        out_specs=pl.BlockSpec((tm, tn), lambda i, j: (i, j)),
        compiler_params=_params(("parallel", "parallel"), est),
        name="ffn_up",
    )(h, wg, wu)


def _mm_resid_kernel(*refs, kblocks, coef, n_next, ntiles, nchunk):
    n_lhs = len(kblocks)
    a_refs = refs[:n_lhs]
    w_ref, x_ref, gp_ref = refs[n_lhs:n_lhs + 3]
    gn_refs = refs[n_lhs + 3:n_lhs + 3 + n_next]
    xo_ref = refs[n_lhs + 3 + n_next]
    h_refs = refs[n_lhs + 4 + n_next:-2]
    acc_refs = refs[-2:]
    i, k = pl.program_id(0), pl.program_id(1)
    rows = xo_ref.shape[0]

    @pl.when((i == 0) & (k == 0))
    def _():
        for acc_ref in acc_refs:
            acc_ref[...] = jnp.zeros_like(acc_ref)

    def accumulate(acc_ref):
        a = a_refs[0][...]
        start = kblocks[0]
        for a_ref, nb in zip(a_refs[1:], kblocks[1:]):
            a = jnp.where(k >= start, a_ref[...], a)
            start += nb
        prev = jnp.where(k == 0, 0.0, acc_ref[...])
        acc_ref[...] = prev + jnp.dot(a, w_ref[...], preferred_element_type=F32)

    def epilogue(acc_ref):
        r0 = pl.multiple_of(jnp.minimum(k, nchunk - 1) * rows, rows)
        y = acc_ref[pl.ds(r0, rows), :]
        xn = x_ref[...] + coef * (_rms_normalize(y) * gp_ref[...])
        xo_ref[...] = xn
        if n_next:
            r = _rms_normalize(xn)
            for gn_ref, h_ref in zip(gn_refs, h_refs):
                h_ref[...] = (r * gn_ref[...]).astype(h_ref.dtype)

    for parity in range(2):
        cur, prev = acc_refs[parity], acc_refs[1 - parity]

        @pl.when((i % 2 == parity) & (i < ntiles))
        def _(cur=cur, prev=prev):
            accumulate(cur)
            epilogue(prev)

    @pl.when(i == ntiles)
    def _():
        epilogue(acc_refs[1 - ntiles % 2])


def mm_resid(lhs, w, x, g_post, coef, g_next):
    m, d = x.shape
    widths = [a.shape[1] for a in lhs]
    assert sum(widths) == w.shape[0] and w.shape[1] == d
    tk = RESID_TILE_K
    while any(kw % tk for kw in widths):
        tk //= 2
    tm = _tile(m, RESID_TILE_M)
    ntiles = m // tm
    kblocks = tuple(kw // tk for kw in widths)
    starts = [sum(kblocks[:j]) for j in range(len(lhs))]
    n_next = len(g_next)
    nk = sum(kblocks)
    nchunk = 1
    while nchunk * 2 <= min(nk, RESID_MAX_CHUNKS) and tm % (nchunk * 2 * BF16_ROWS) == 0:
        nchunk *= 2
    rows = tm // nchunk
    gain = pl.BlockSpec((1, d), lambda i, k: (0, 0))

    def step(i, k):
        return jnp.where(i == ntiles, nk - 1, k)

    def lhs_spec(start, nb):
        return pl.BlockSpec(
            (tm, tk),
            lambda i, k: (jnp.minimum(i, ntiles - 1), jnp.clip(step(i, k) - start, 0, nb - 1)))

    chunk = pl.BlockSpec(
        (rows, d), lambda i, k: (jnp.maximum((i - 1) * nchunk + jnp.minimum(k, nchunk - 1), 0), 0))
    est = len(lhs) * 2 * tm * tk * 2 + 2 * tk * d * 2 + 2 * tm * d * 4 \
        + 2 * rows * d * (8 + 2 * n_next) + 8 * rows * d * 4 + tm * tk * 2
    outs = pl.pallas_call(
        functools.partial(_mm_resid_kernel, kblocks=kblocks, coef=coef, n_next=n_next,
                          ntiles=ntiles, nchunk=nchunk),
        out_shape=[jax.ShapeDtypeStruct((m, d), F32)]
        + [jax.ShapeDtypeStruct((m, d), BF16) for _ in range(n_next)],
        grid=(ntiles + 1, nk),
        in_specs=[lhs_spec(s, nb) for s, nb in zip(starts, kblocks)]
        + [pl.BlockSpec((tk, d), lambda i, k: (step(i, k), 0)), chunk, gain] + [gain] * n_next,
        out_specs=[chunk] * (1 + n_next),
        scratch_shapes=[pltpu.VMEM((tm, d), F32), pltpu.VMEM((tm, d), F32)],
        compiler_params=_params(("arbitrary", "arbitrary"), est),
        name="mm_resid",
    )(*lhs, w, x, g_post.reshape(1, d).astype(F32), *[g.reshape(1, d).astype(F32) for g in g_next])
    return outs[0], outs[1:]


def _pool_kernel(u_ref, halo_ref, prev_ref, w_ref, sc_ref, o_ref, ext_ref, *, tm, pos0, group):
    i = pl.program_id(1)

    @pl.when(i == 0)
    def _():
        ext_ref[0:HALO, :] = prev_ref[0]

    @pl.when(i > 0)
    def _():
        ext_ref[0:HALO, :] = halo_ref[0]

    ext_ref[HALO:HALO + tm, :] = u_ref[0]
    pos = pos0 + i * tm + lax.broadcasted_iota(jnp.int32, (tm, 1), 0)
    for gi, win in enumerate(POOL_WINDOWS):
        sl = slice(gi * group, (gi + 1) * group)
        ug = ext_ref[HALO:HALO + tm, sl]
        wsum = ug
        for j in range(1, win):
            wsum = wsum + ext_ref[HALO - j:HALO - j + tm, sl]
        cnt = jnp.minimum(pos + 1, win).astype(F32)
        diff = wsum / cnt - ug
        y = jnp.dot(diff.astype(BF16), w_ref[gi], preferred_element_type=F32)
        o_ref[0, :, sl] = (y * sc_ref[:, sl]).astype(o_ref.dtype)


def pool_mixer(u, prev, pos0, pool_w, pool_scale):
    b, n, c = u.shape
    group = c // len(POOL_WINDOWS)
    tm = _tile(n, POOL_TILE)
    steps = tm // HALO
    est = 2 * (tm * c * 4 + 2 * HALO * c * 4 + pool_w.size * 2 + tm * c * 2) + (tm + HALO) * c * 4 \
        + 4 * tm * group * 4
    return pl.pallas_call(
        functools.partial(_pool_kernel, tm=tm, pos0=pos0, group=group),
        out_shape=jax.ShapeDtypeStruct((b, n, c), BF16),
        grid=(b, n // tm),
        in_specs=[pl.BlockSpec((1, tm, c), lambda bb, i: (bb, i, 0)),
                  pl.BlockSpec((1, HALO, c), lambda bb, i: (bb, jnp.maximum(i * steps - 1, 0), 0)),
                  pl.BlockSpec((1, HALO, c), lambda bb, i: (bb, 0, 0)),
                  pl.BlockSpec(pool_w.shape, lambda bb, i: (0, 0, 0)),
                  pl.BlockSpec((1, c), lambda bb, i: (0, 0))],
        out_specs=pl.BlockSpec((1, tm, c), lambda bb, i: (bb, i, 0)),
        scratch_shapes=[pltpu.VMEM((tm + HALO, c), F32)],
        compiler_params=_params(("parallel", "parallel"), est),
        name="pool_mixer",
    )(u, u, prev, pool_w, pool_scale.reshape(1, c).astype(F32))


def _mem_attn_kernel(q_ref, k_ref, v_ref, o_ref, *, heads, hd):
    scale = hd ** -0.5
    for hh in range(heads):
        sl = slice(hh * hd, (hh + 1) * hd)
        q = q_ref[0, :, sl]
        k = k_ref[0, :, sl].astype(BF16)
        v = v_ref[0, :, sl].astype(BF16)
        s = lax.dot_general(q, k, (((1,), (1,)), ((), ())), preferred_element_type=F32) * scale
        p = jnp.exp(s - jnp.max(s, axis=-1, keepdims=True))
        p = p / jnp.sum(p, axis=-1, keepdims=True)
        o_ref[0, :, sl] = jnp.dot(p.astype(BF16), v, preferred_element_type=F32).astype(o_ref.dtype)


def mem_attend(qm, mk, mv, heads):
    b, n, dq = qm.shape
    nm = mk.shape[1]
    tm = _tile(n, ROW_TILE)
    est = 2 * (2 * tm * dq * 2 + 2 * nm * dq * 4) + 4 * tm * nm * 4
    return pl.pallas_call(
        functools.partial(_mem_attn_kernel, heads=heads, hd=dq // heads),
        out_shape=jax.ShapeDtypeStruct((b, n, dq), BF16),
        grid=(b, n // tm),
        in_specs=[pl.BlockSpec((1, tm, dq), lambda bb, i: (bb, i, 0)),
                  pl.BlockSpec((1, nm, dq), lambda bb, i: (bb, 0, 0)),
                  pl.BlockSpec((1, nm, dq), lambda bb, i: (bb, 0, 0))],
        out_specs=pl.BlockSpec((1, tm, dq), lambda bb, i: (bb, i, 0)),
        compiler_params=_params(("parallel", "parallel"), est),
        name="mem_attend",
    )(qm, mk, mv)


def _logf_kernel(h_ref, w_ref, b_ref, o_ref):
    z = jnp.dot(h_ref[...], w_ref[...], preferred_element_type=F32) + b_ref[...]
    o_ref[...] = jnp.minimum(z, 0.0) - jnp.log1p(jnp.exp(-jnp.abs(z)))


def log_forget(hk, w_f, b_f):
    m, k = hk.shape
    nh = w_f.shape[1]
    wp = jnp.pad(w_f, ((0, 0), (0, LANE - nh))).astype(BF16)
    bp = jnp.pad(b_f, (0, LANE - nh)).reshape(1, LANE).astype(F32)
    tm = _tile(m, MM_TILE)
    out = pl.pallas_call(
        _logf_kernel,
        out_shape=jax.ShapeDtypeStruct((m, LANE), F32),
        grid=(m // tm,),
        in_specs=[pl.BlockSpec((tm, k), lambda i: (i, 0)),
                  pl.BlockSpec((k, LANE), lambda i: (0, 0)),
                  pl.BlockSpec((1, LANE), lambda i: (0, 0))],
        out_specs=pl.BlockSpec((tm, LANE), lambda i: (i, 0)),
        compiler_params=_params(("parallel",), 2 * (tm * k * 2 + k * LANE * 2 + tm * LANE * 4)),
        name="log_forget",
    )(hk, wp, bp)
    return out[:, :nh]


def _cumsum_kernel(x_ref, o_ref, carry_ref, *, tb):
    @pl.when(pl.program_id(1) == 0)
    def _():
        carry_ref[...] = jnp.zeros_like(carry_ref)

    x = x_ref[0]
    row = lax.broadcasted_iota(jnp.int32, x.shape, 0)
    d = 1
    while d < tb:
        x = x + jnp.where(row >= d, pltpu.roll(x, d, 0), 0.0)
        d *= 2
    x = x + carry_ref[0:1, :]
    o_ref[0] = x
    carry_ref[...] = jnp.broadcast_to(x[tb - 1:tb, :], carry_ref.shape)


def cumsum_time(logf):
    b, t, nh = logf.shape
    tb = CUMSUM_TILE
    tp = -(-t // tb) * tb
    x = jnp.pad(logf, ((0, 0), (0, tp - t), (0, LANE - nh)))
    out = pl.pallas_call(
        functools.partial(_cumsum_kernel, tb=tb),
        out_shape=jax.ShapeDtypeStruct((b, tp, LANE), F32),
        grid=(b, tp // tb),
        in_specs=[pl.BlockSpec((1, tb, LANE), lambda bb, i: (bb, i, 0))],
        out_specs=pl.BlockSpec((1, tb, LANE), lambda bb, i: (bb, i, 0)),
        scratch_shapes=[pltpu.VMEM((8, LANE), F32)],
        compiler_params=_params(("parallel", "arbitrary"), 8 * tb * LANE * 4),
        name="cumsum_time",
    )(x)
    return out[:, :t, :nh]


LOG2E = 1.4426950408889634
_NT = (((1,), (1,)), ((), ()))


def _lanes(x, reps):
    return x if reps == 1 else jnp.concatenate([x] * reps, axis=1)


def _fox_prompt_kernel(q_ref, k_ref, v_ref, f_ref, o_ref, s0_ref, s1_ref, m_ref, acc_ref,
                       *, tq, tk, hd, nhb):
    qi = pl.program_id(1)
    m_ref[...] = jnp.full_like(m_ref, MASK_VALUE)
    acc_ref[...] = jnp.zeros_like(acc_ref)
    ones = jnp.ones((tk, hd), BF16)

    def logits(kb, s_ref, r0=0):
        off = pl.multiple_of(kb * tk, tk)
        for hh in range(nhb):
            sl = slice(hh * hd, (hh + 1) * hd)
            s = lax.dot_general(q_ref[r0:, sl], k_ref[pl.ds(off, tk), sl], _NT,
                                preferred_element_type=F32)
            s_ref[hh, r0:, :] = s - f_ref[hh, kb] * LOG2E

    def attend(kb, s_ref, diag=None):
        off = pl.multiple_of(kb * tk, tk)
        r0 = 0 if diag is None else diag * tk
        for hh in range(nhb):
            sl = slice(hh * hd, (hh + 1) * hd)
            s = s_ref[hh, r0:, :]
            if diag is not None:
                row = lax.broadcasted_iota(jnp.int32, s.shape, 0)
                col = lax.broadcasted_iota(jnp.int32, s.shape, 1)
                s = jnp.where(col <= row, s, MASK_VALUE)
            m_prev = m_ref[hh, r0:, :]
            m_new = jnp.maximum(m_prev, jnp.max(s, axis=-1, keepdims=True))
            alpha = jnp.exp2(m_prev - m_new)
            p = jnp.exp2(s - _lanes(m_new, tk // LANE))
            v = v_ref[pl.ds(off, tk), sl]
            pv = jnp.dot(p.astype(BF16), jnp.concatenate([v, ones], axis=1),
                         preferred_element_type=F32)
            acc_ref[hh, r0:, :] = _lanes(alpha, 2 * hd // LANE) * acc_ref[hh, r0:, :] + pv
            m_ref[hh, r0:, :] = m_new

    def body(jj, carry):
        kb = 2 * jj
        logits(kb + 1, s1_ref)
        attend(kb, s0_ref)
        logits(kb + 2, s0_ref)
        attend(kb + 1, s1_ref)
        return carry

    logits(0, s0_ref)
    lax.fori_loop(0, qi, body, 0)
    logits(2 * qi + 1, s1_ref, r0=tk)
    attend(2 * qi, s0_ref, diag=0)
    attend(2 * qi + 1, s1_ref, diag=1)

    for hh in range(nhb):
        acc = acc_ref[hh]
        o_ref[:, hh * hd:(hh + 1) * hd] = (acc[:, :hd] / acc[:, hd:]).astype(o_ref.dtype)


def fox_prompt(q, k, v, f_cum, hd):
    n, dm = q.shape
    nh = dm // hd
    nhb = 2 if nh % 2 == 0 else 1
    tq = _tile(n, FOX_TILE_Q)
    tk = tq // 2
    assert hd == LANE and tk % LANE == 0
    f_t = f_cum.T.reshape(nh, n // tk, 1, tk)
    wb = nhb * hd
    est = 2 * (2 * tq * wb * 2 + 2 * n * wb * 2 + nhb * (n // tk) * 8 * tk * 4) \
        + nhb * (tq * LANE * 4 + tq * 2 * hd * 4 + 5 * tq * tk * 4)
    return pl.pallas_call(
        functools.partial(_fox_prompt_kernel, tq=tq, tk=tk, hd=hd, nhb=nhb),
        out_shape=jax.ShapeDtypeStruct((n, dm), BF16),
        grid=(nh // nhb, n // tq),
        in_specs=[pl.BlockSpec((tq, wb), lambda h, i: (i, h)),
                  pl.BlockSpec((n, wb), lambda h, i: (0, h)),
                  pl.BlockSpec((n, wb), lambda h, i: (0, h)),
                  pl.BlockSpec((nhb, n // tk, 1, tk), lambda h, i: (h, 0, 0, 0))],
        out_specs=pl.BlockSpec((tq, wb), lambda h, i: (i, h)),
        scratch_shapes=[pltpu.VMEM((nhb, tq, tk), F32), pltpu.VMEM((nhb, tq, tk), F32),
                        pltpu.VMEM((nhb, tq, LANE), F32), pltpu.VMEM((nhb, tq, 2 * hd), F32)],
        compiler_params=_params(("parallel", "parallel"), est),
        name="fox_prompt",
    )(q, k, v, f_t)


def _fox_cached_kernel(q_ref, kp_ref, vp_ref, kn_ref, vn_ref, fp_ref, fn_ref, rh_ref, rt_ref,
                       ch_ref, ct_ref, o_ref, m_ref, acc_ref, *, sub, hd):
    j = pl.program_id(1)
    q = q_ref[0]
    rh = rh_ref[...]

    @pl.when(j == 0)
    def _():
        m_ref[...] = jnp.full_like(m_ref, MASK_VALUE)
        acc_ref[...] = jnp.zeros_like(acc_ref)

    def logits(k, f, valid):
        s = lax.dot_general(q, k, _NT, preferred_element_type=F32) - f * LOG2E
        return jnp.where(valid, s, MASK_VALUE)

    def absorb(s, v):
        width = s.shape[1]
        m_prev = m_ref[...]
        m_new = jnp.maximum(m_prev, jnp.max(s, axis=-1, keepdims=True))
        alpha = jnp.exp2(m_prev - m_new)
        p = jnp.exp2(s - (_lanes(m_new, width // LANE) if width % LANE == 0 else m_new[:, :1]))
        vaug = jnp.concatenate([v, jnp.ones_like(v)], axis=1)
        pv = jnp.dot(p.astype(BF16), vaug, preferred_element_type=F32)
        acc_ref[...] = _lanes(alpha, 2 * hd // LANE) * acc_ref[...] + pv
        m_ref[...] = m_new

    subs = [slice(c0, c0 + sub) for c0 in range(0, kp_ref.shape[1], sub)]
    s_all = [logits(kp_ref[0, cs, :].astype(BF16), fp_ref[0, :, cs], rh == ch_ref[:, cs])
             for cs in subs]
    for s, cs in zip(s_all, subs):
        absorb(s, vp_ref[0, cs, :].astype(BF16))

    @pl.when(j == pl.num_programs(1) - 1)
    def _():
        r = kn_ref.shape[1]
        absorb(logits(kn_ref[0], fn_ref[0], (rh == ch_ref[:, :r]) & (ct_ref[...] <= rt_ref[...])),
               vn_ref[0])
        acc = acc_ref[...]
        o_ref[0] = (acc[:, :hd] / acc[:, hd:]).astype(o_ref.dtype)


def fox_cached(q, k_past, v_past, k_new, v_new, f_all, hd):
    b, n, dm = q.shape
    p, nh = k_past.shape[1:3]
    r = n * nh
    tp = _tile(p, FOX_PAST_TILE)
    c = tp * nh
    sub = c // 2 if c % (2 * LANE * nh) == 0 else c
    assert c % LANE == 0 and c >= r
    heads = jnp.arange(c, dtype=jnp.int32) % nh
    times = jnp.arange(r, dtype=jnp.int32) // nh
    new = pl.BlockSpec((1, r, hd), lambda bb, j: (bb, 0, 0))
    past = pl.BlockSpec((1, c, hd), lambda bb, j: (bb, j, 0))
    whole = lambda shape: pl.BlockSpec(shape, lambda bb, j: (0, 0))
    est = 2 * (2 * c * hd * 4 + 4 * r * hd * 2) + 6 * r * sub * 4 + 3 * r * 2 * hd * 4
    out = pl.pallas_call(
        functools.partial(_fox_cached_kernel, sub=sub, hd=hd),
        out_shape=jax.ShapeDtypeStruct((b, r, hd), BF16),
        grid=(b, p // tp),
        in_specs=[new, past, past, new, new,
                  pl.BlockSpec((1, 1, c), lambda bb, j: (bb, 0, j)),
                  pl.BlockSpec((1, 1, r), lambda bb, j: (bb, 0, 0)),
                  whole((r, 1)), whole((r, 1)), whole((1, c)), whole((1, r))],
        out_specs=new,
        scratch_shapes=[pltpu.VMEM((r, LANE), F32), pltpu.VMEM((r, 2 * hd), F32)],
        compiler_params=_params(("parallel", "arbitrary"), est),
        name="fox_cached",
    )(q.reshape(b, r, hd), k_past.reshape(b, p * nh, hd), v_past.reshape(b, p * nh, hd),
      k_new.reshape(b, r, hd), v_new.reshape(b, r, hd),
      f_all[:, :p].reshape(b, 1, p * nh), f_all[:, p:].reshape(b, 1, r),
      heads[:r].reshape(r, 1), times.reshape(r, 1), heads.reshape(1, c), times.reshape(1, r))
    return out.reshape(b, n, dm)


def _trunk(x, pool_prev, mem_k, mem_v, past, w):
    b, n, d = x.shape
    m = b * n
    depth, n_a = w["depth"], w["n_a"]
    c, hd, heads = w["d_pool"], w["hd_fox"], w["mem_heads"]
    assert n >= POOL_STATE
    pos0 = 0 if past is None else past[0].shape[1]
    x2 = x.reshape(m, d)
    h = rms_cast(x2, w["norm_g"][0, 0])
    new_pool = []
    k_new = v_new = logf_new = None
    k_bf = v_bf = f_all = None
    for l in range(depth):
        g = w["norm_g"][l]
        a = ffn_up(h, w["wg"][l][0], w["wu"][l][0])
        x2, (h,) = mm_resid([a], w["wd"][l][0], x2, g[1], 0.5, [g[2]])
        if l < n_a:
            w_in, w_out = w["w_in_a"][l], w["w_out_a"][l]
            (u,) = mm(h, w_in, [F32], cols=(0, c))
            u3 = u.reshape(b, n, c)
            main = pool_mixer(u3, pool_prev[l], pos0, w["pool_w"][l], w["pool_scale"][l])
            new_pool.append(u3[:, n - POOL_STATE:, :])
        else:
            j = l - n_a
            w_in, w_out = w["w_in_b"][j], w["w_out_b"][j]
            (q,) = mm(h, w_in, [BF16], scale=LOG2E * hd ** -0.5, cols=(0, c))
            if past is None:
                main = fox_prompt(q, k_bf, v_bf, f_all[0], hd).reshape(b, n, c)
            else:
                main = fox_cached(q.reshape(b, n, c), past[0], past[1], k_bf.reshape(b, n, c),
                                  v_bf.reshape(b, n, c), f_all, hd)
        (qm,) = mm(h, w_in, [BF16], cols=(c, d))
        mem = mem_attend(qm.reshape(b, n, d - c), mem_k[l], mem_v[l], heads)
        x2, (h,) = mm_resid([main.reshape(m, c), mem.reshape(m, d - c)], w_out, x2, g[3], 1.0, [g[4]])
        a = ffn_up(h, w["wg"][l][1], w["wu"][l][1])
        g_next = []
        if l + 1 < depth:
            g_next.append(w["norm_g"][l + 1, 0])
        if l == n_a - 1:
            g_next.append(w["g_kv"])
        x2, hs = mm_resid([a], w["wd"][l][1], x2, g[5], 0.5, g_next)
        if l + 1 < depth:
            h = hs[0]
        if l == n_a - 1:
            hk = hs[-1]
            k_new, k_bf = mm(hk, w["w_kv"], [F32, BF16], cols=(0, c))
            v_new, v_bf = mm(hk, w["w_kv"], [F32, BF16], cols=(c, 2 * c))
            logf_new = log_forget(hk, w["w_f"], w["b_f"]).reshape(b, n, -1)
            logf_all = logf_new if past is None else jnp.concatenate(
                [past[2].astype(F32), logf_new], axis=1)
            f_all = cumsum_time(logf_all)
    nh = c // hd
    return (x2.reshape(b, n, d), jnp.stack(new_pool), k_new.reshape(b, n, nh, hd),
            v_new.reshape(b, n, nh, hd), logf_new)


def _cast_pad_kernel(x_ref, o_ref, *, rows, cols):
    tr, tc = o_ref.shape
    r = pl.program_id(0) * tr + lax.broadcasted_iota(jnp.int32, (tr, tc), 0)
    c = pl.program_id(1) * tc + lax.broadcasted_iota(jnp.int32, (tr, tc), 1)
    o_ref[...] = jnp.where((r < rows) & (c < cols), x_ref[...], 0.0).astype(o_ref.dtype)


def _pad_ff(w4, l, s, axis):
    rows, cols = w4.shape[2:]
    shape = [rows, cols]
    shape[axis] += -shape[axis] % FF_ALIGN
    tr, tc = _tile(shape[0], MM_TILE), _tile(shape[1], MM_TILE)
    return pl.pallas_call(
        functools.partial(_cast_pad_kernel, rows=rows, cols=cols),
        out_shape=jax.ShapeDtypeStruct(tuple(shape), BF16),
        grid=(shape[0] // tr, shape[1] // tc),
        in_specs=[pl.BlockSpec((None, None, tr, tc), lambda i, j: (l, s, i, j))],
        out_specs=pl.BlockSpec((tr, tc), lambda i, j: (i, j)),
        compiler_params=_params(("parallel", "parallel"), 2 * tr * tc * 6 + 3 * tr * tc * 4),
        name="cast_pad",
    )(w4)


def kernel(x_prompt, x_sample, mem_prompt, cache_fox_k, cache_fox_v, cache_fox_logf, cache_mem_k, cache_mem_v, state_pool, norm_g, w_ffn_gate, w_ffn_up, w_ffn_down, w_in_a, pool_w, pool_scale, w_out_a, w_in_b, w_out_b, g_kv, w_kv, w_f, b_f, g_mem, w_mem_kv):
    depth = norm_g.shape[0]
    n_a, db, _, c = state_pool.shape
    hd = cache_fox_k.shape[3]
    _, _, n_mem, heads, mem_hd = cache_mem_k.shape
    dq = heads * mem_hd
    bp, _, d = x_prompt.shape
    w = dict(
        depth=depth, n_a=n_a, d_pool=c, hd_fox=hd, mem_heads=heads,
        norm_g=norm_g, g_kv=g_kv, w_f=w_f, b_f=b_f, pool_scale=pool_scale,
        wg=[[_pad_ff(w_ffn_gate, l, s, 1) for s in range(2)] for l in range(depth)],
        wu=[[_pad_ff(w_ffn_up, l, s, 1) for s in range(2)] for l in range(depth)],
        wd=[[_pad_ff(w_ffn_down, l, s, 0) for s in range(2)] for l in range(depth)],
        w_in_a=w_in_a.astype(BF16), w_out_a=w_out_a.astype(BF16),
        w_in_b=w_in_b.astype(BF16), w_out_b=w_out_b.astype(BF16),
        pool_w=pool_w.astype(BF16), w_kv=w_kv.astype(BF16),
    )
    mem2 = mem_prompt.reshape(bp * n_mem, d)
    mem_kv = [mm(rms_cast(mem2, g_mem[l]), w_mem_kv[l].astype(BF16), [F32])[0] for l in range(depth)]
    mem_k_p = jnp.stack([kv[:, :dq].reshape(bp, n_mem, dq) for kv in mem_kv])
    mem_v_p = jnp.stack([kv[:, dq:].reshape(bp, n_mem, dq) for kv in mem_kv])

    zero_prev = jnp.zeros((n_a, bp, HALO, c), F32)
    y_p, pool_p, k_p, v_p, logf_p = _trunk(x_prompt, zero_prev, mem_k_p, mem_v_p, None, w)

    prev_s = jnp.pad(state_pool.astype(F32), ((0, 0), (0, 0), (1, 0), (0, 0)))
    past = (cache_fox_k, cache_fox_v, cache_fox_logf)
    y_s, pool_s, k_s, v_s, logf_s = _trunk(
        x_sample, prev_s, cache_mem_k.reshape(depth, db, n_mem, dq),
        cache_mem_v.reshape(depth, db, n_mem, dq), past, w)

    return (y_p, y_s, k_p, v_p, logf_p,
            mem_k_p.reshape(depth, bp, n_mem, heads, mem_hd),
            mem_v_p.reshape(depth, bp, n_mem, heads, mem_hd),
            pool_p, k_s, v_s, logf_s, pool_s)
```

```python
import functools

import jax
import jax.numpy as jnp
from jax import lax
from jax.experimental import pallas as pl
from jax.experimental.pallas import tpu as pltpu

EPS = 1e-6
POOL_WINDOWS = (2, 4, 8, 16)
POOL_STATE = max(POOL_WINDOWS) - 1
HALO = POOL_STATE + 1
LANE = 128
BF16_ROWS = 16
MASK_VALUE = -1e30
VMEM_CAP = 60 * 1024 * 1024
VMEM_FLOOR = 32 * 1024 * 1024
VMEM_SLACK = 4 * 1024 * 1024
F32 = jnp.float32
BF16 = jnp.bfloat16

MM_TILE = 1024
FFN_UP_TILE_N = 512
FF_ALIGN = 1024
RESID_TILE_M, RESID_TILE_K = 512, 1024
RESID_MAX_CHUNKS = 8
ROW_TILE = 512
CAST_TILE_ELEMS = 2 * 1024 * 1024
POOL_TILE = 512
FOX_TILE_Q = 1024
FOX_PAST_TILE = 256
CUMSUM_TILE = 256


def _tile(n, pref):
    if n <= pref:
        return n
    t = pref
    while n % t:
        t //= 2
    return t


def _params(sem, est_bytes):
    limit = int(min(VMEM_CAP, max(VMEM_FLOOR, est_bytes * 5 // 4 + VMEM_SLACK)))
    return pltpu.CompilerParams(dimension_semantics=sem, vmem_limit_bytes=limit)


def _rms_normalize(xf):
    return xf * lax.rsqrt(jnp.mean(xf * xf, axis=-1, keepdims=True) + EPS)


def _rms_cast_kernel(x_ref, g_ref, o_ref):
    o_ref[...] = (_rms_normalize(x_ref[...]) * g_ref[...]).astype(o_ref.dtype)


def rms_cast(x, g):
    m, d = x.shape
    tm = _tile(m, ROW_TILE)
    return pl.pallas_call(
        _rms_cast_kernel,
        out_shape=jax.ShapeDtypeStruct((m, d), BF16),
        grid=(m // tm,),
        in_specs=[pl.BlockSpec((tm, d), lambda i: (i, 0)),
                  pl.BlockSpec((1, d), lambda i: (0, 0))],
        out_specs=pl.BlockSpec((tm, d), lambda i: (i, 0)),
        compiler_params=_params(("parallel",), 2 * tm * d * 6),
        name="rms_cast",
    )(x, g.reshape(1, d).astype(F32))


def _mm_kernel(h_ref, w_ref, *o_refs, scale):
    acc = jnp.dot(h_ref[...], w_ref[...], preferred_element_type=F32)
    if scale is not None:
        acc = acc * scale
    for o_ref in o_refs:
        o_ref[...] = acc.astype(o_ref.dtype)


def mm(h, w, out_dtypes, scale=None, cols=None):
    m, k = h.shape
    c0, c1 = cols if cols is not None else (0, w.shape[1])
    n = c1 - c0
    tm, tn = _tile(m, MM_TILE), _tile(n, MM_TILE)
    assert c0 % tn == 0
    j0 = c0 // tn
    est = 2 * (tm * k * 2 + k * tn * 2) + tm * tn * 4 + sum(
        2 * tm * tn * jnp.dtype(t).itemsize for t in out_dtypes)
    outs = pl.pallas_call(
        functools.partial(_mm_kernel, scale=scale),
        out_shape=[jax.ShapeDtypeStruct((m, n), t) for t in out_dtypes],
        grid=(m // tm, n // tn),
        in_specs=[pl.BlockSpec((tm, k), lambda i, j: (i, 0)),
                  pl.BlockSpec((k, tn), lambda i, j: (0, j + j0))],
        out_specs=[pl.BlockSpec((tm, tn), lambda i, j: (i, j)) for _ in out_dtypes],
        compiler_params=_params(("parallel", "parallel"), est),
        name="mm",
    )(h, w)
    return outs


def _ffn_up_kernel(h_ref, wg_ref, wu_ref, o_ref):
    h = h_ref[...]
    g = jnp.dot(h, wg_ref[...], preferred_element_type=F32)
    u = jnp.dot(h, wu_ref[...], preferred_element_type=F32)
    o_ref[...] = (g * jax.nn.sigmoid(g) * u).astype(o_ref.dtype)


def ffn_up(h, wg, wu):
    m, k = h.shape
    n = wg.shape[1]
    tm, tn = _tile(m, MM_TILE), _tile(n, FFN_UP_TILE_N)
    est = 2 * (tm * k * 2 + 2 * k * tn * 2 + tm * tn * 2) + 3 * tm * tn * 4
    return pl.pallas_call(
        _ffn_up_kernel,
        out_shape=jax.ShapeDtypeStruct((m, n), BF16),
        grid=(m // tm, n // tn),
        in_specs=[pl.BlockSpec((tm, k), lambda i, j: (i, 0)),
                  pl.BlockSpec((k, tn), lambda i, j: (0, j)),
                  pl.BlockSpec((k, tn), lambda i, j: (0, j))],
        out_specs=pl.BlockSpec((tm, tn), lambda i, j: (i, j)),
        compiler_params=_params(("parallel", "parallel"), est),
        name="ffn_up",
    )(h, wg, wu)


def _mm_resid_kernel(*refs, kblocks, coef, n_next, ntiles, nchunk):
    n_lhs = len(kblocks)
    a_refs = refs[:n_lhs]
    w_ref, x_ref, gp_ref = refs[n_lhs:n_lhs + 3]
    gn_refs = refs[n_lhs + 3:n_lhs + 3 + n_next]
    xo_ref = refs[n_lhs + 3 + n_next]
    h_refs = refs[n_lhs + 4 + n_next:-2]
    acc_refs = refs[-2:]
    i, k = pl.program_id(0), pl.program_id(1)
    rows = xo_ref.shape[0]

    @pl.when((i == 0) & (k == 0))
    def _():
        for acc_ref in acc_refs:
            acc_ref[...] = jnp.zeros_like(acc_ref)

    def accumulate(acc_ref):
        a = a_refs[0][...]
        start = kblocks[0]
        for a_ref, nb in zip(a_refs[1:], kblocks[1:]):
            a = jnp.where(k >= start, a_ref[...], a)
            start += nb
        prev = jnp.where(k == 0, 0.0, acc_ref[...])
        acc_ref[...] = prev + jnp.dot(a, w_ref[...], preferred_element_type=F32)

    def epilogue(acc_ref):
        r0 = pl.multiple_of(jnp.minimum(k, nchunk - 1) * rows, rows)
        y = acc_ref[pl.ds(r0, rows), :]
        xn = x_ref[...] + coef * (_rms_normalize(y) * gp_ref[...])
        xo_ref[...] = xn
        if n_next:
            r = _rms_normalize(xn)
            for gn_ref, h_ref in zip(gn_refs, h_refs):
                h_ref[...] = (r * gn_ref[...]).astype(h_ref.dtype)

    for parity in range(2):
        cur, prev = acc_refs[parity], acc_refs[1 - parity]

        @pl.when((i % 2 == parity) & (i < ntiles))
        def _(cur=cur, prev=prev):
            accumulate(cur)
            epilogue(prev)

    @pl.when(i == ntiles)
    def _():
        epilogue(acc_refs[1 - ntiles % 2])


def mm_resid(lhs, w, x, g_post, coef, g_next):
    m, d = x.shape
    widths = [a.shape[1] for a in lhs]
    assert sum(widths) == w.shape[0] and w.shape[1] == d
    tk = RESID_TILE_K
    while any(kw % tk for kw in widths):
        tk //= 2
    tm = _tile(m, RESID_TILE_M)
    ntiles = m // tm
    kblocks = tuple(kw // tk for kw in widths)
    starts = [sum(kblocks[:j]) for j in range(len(lhs))]
    n_next = len(g_next)
    nk = sum(kblocks)
    nchunk = 1
    while nchunk * 2 <= min(nk, RESID_MAX_CHUNKS) and tm % (nchunk * 2 * BF16_ROWS) == 0:
        nchunk *= 2
    rows = tm // nchunk
    gain = pl.BlockSpec((1, d), lambda i, k: (0, 0))

    def step(i, k):
        return jnp.where(i == ntiles, nk - 1, k)

    def lhs_spec(start, nb):
        return pl.BlockSpec(
            (tm, tk),
            lambda i, k: (jnp.minimum(i, ntiles - 1), jnp.clip(step(i, k) - start, 0, nb - 1)))

    chunk = pl.BlockSpec(
        (rows, d), lambda i, k: (jnp.maximum((i - 1) * nchunk + jnp.minimum(k, nchunk - 1), 0), 0))
    est = len(lhs) * 2 * tm * tk * 2 + 2 * tk * d * 2 + 2 * tm * d * 4 \
        + 2 * rows * d * (8 + 2 * n_next) + 8 * rows * d * 4 + tm * tk * 2
    outs = pl.pallas_call(
        functools.partial(_mm_resid_kernel, kblocks=kblocks, coef=coef, n_next=n_next,
                          ntiles=ntiles, nchunk=nchunk),
        out_shape=[jax.ShapeDtypeStruct((m, d), F32)]
        + [jax.ShapeDtypeStruct((m, d), BF16) for _ in range(n_next)],
        grid=(ntiles + 1, nk),
        in_specs=[lhs_spec(s, nb) for s, nb in zip(starts, kblocks)]
        + [pl.BlockSpec((tk, d), lambda i, k: (step(i, k), 0)), chunk, gain] + [gain] * n_next,
        out_specs=[chunk] * (1 + n_next),
        scratch_shapes=[pltpu.VMEM((tm, d), F32), pltpu.VMEM((tm, d), F32)],
        compiler_params=_params(("arbitrary", "arbitrary"), est),
        name="mm_resid",
    )(*lhs, w, x, g_post.reshape(1, d).astype(F32), *[g.reshape(1, d).astype(F32) for g in g_next])
    return outs[0], outs[1:]


def _pool_kernel(u_ref, halo_ref, prev_ref, w_ref, sc_ref, o_ref, ext_ref, *, tm, pos0, group):
    i = pl.program_id(1)

    @pl.when(i == 0)
    def _():
        ext_ref[0:HALO, :] = prev_ref[0]

    @pl.when(i > 0)
    def _():
        ext_ref[0:HALO, :] = halo_ref[0]

    ext_ref[HALO:HALO + tm, :] = u_ref[0]
    pos = pos0 + i * tm + lax.broadcasted_iota(jnp.int32, (tm, 1), 0)
    for gi, win in enumerate(POOL_WINDOWS):
        sl = slice(gi * group, (gi + 1) * group)
        ug = ext_ref[HALO:HALO + tm, sl]
        wsum = ug
        for j in range(1, win):
            wsum = wsum + ext_ref[HALO - j:HALO - j + tm, sl]
        cnt = jnp.minimum(pos + 1, win).astype(F32)
        diff = wsum / cnt - ug
        y = jnp.dot(diff.astype(BF16), w_ref[gi], preferred_element_type=F32)
        o_ref[0, :, sl] = (y * sc_ref[:, sl]).astype(o_ref.dtype)


def pool_mixer(u, prev, pos0, pool_w, pool_scale):
    b, n, c = u.shape
    group = c // len(POOL_WINDOWS)
    tm = _tile(n, POOL_TILE)
    steps = tm // HALO
    est = 2 * (tm * c * 4 + 2 * HALO * c * 4 + pool_w.size * 2 + tm * c * 2) + (tm + HALO) * c * 4 \
        + 4 * tm * group * 4
    return pl.pallas_call(
        functools.partial(_pool_kernel, tm=tm, pos0=pos0, group=group),
        out_shape=jax.ShapeDtypeStruct((b, n, c), BF16),
        grid=(b, n // tm),
        in_specs=[pl.BlockSpec((1, tm, c), lambda bb, i: (bb, i, 0)),
                  pl.BlockSpec((1, HALO, c), lambda bb, i: (bb, jnp.maximum(i * steps - 1, 0), 0)),
                  pl.BlockSpec((1, HALO, c), lambda bb, i: (bb, 0, 0)),
                  pl.BlockSpec(pool_w.shape, lambda bb, i: (0, 0, 0)),
                  pl.BlockSpec((1, c), lambda bb, i: (0, 0))],
        out_specs=pl.BlockSpec((1, tm, c), lambda bb, i: (bb, i, 0)),
        scratch_shapes=[pltpu.VMEM((tm + HALO, c), F32)],
        compiler_params=_params(("parallel", "parallel"), est),
        name="pool_mixer",
    )(u, u, prev, pool_w, pool_scale.reshape(1, c).astype(F32))


def _mem_attn_kernel(q_ref, k_ref, v_ref, o_ref, *, heads, hd):
    scale = hd ** -0.5
    for hh in range(heads):
        sl = slice(hh * hd, (hh + 1) * hd)
        q = q_ref[0, :, sl]
        k = k_ref[0, :, sl].astype(BF16)
        v = v_ref[0, :, sl].astype(BF16)
        s = lax.dot_general(q, k, (((1,), (1,)), ((), ())), preferred_element_type=F32) * scale
        p = jnp.exp(s - jnp.max(s, axis=-1, keepdims=True))
        p = p / jnp.sum(p, axis=-1, keepdims=True)
        o_ref[0, :, sl] = jnp.dot(p.astype(BF16), v, preferred_element_type=F32).astype(o_ref.dtype)


def mem_attend(qm, mk, mv, heads):
    b, n, dq = qm.shape
    nm = mk.shape[1]
    tm = _tile(n, ROW_TILE)
    est = 2 * (2 * tm * dq * 2 + 2 * nm * dq * 4) + 4 * tm * nm * 4
    return pl.pallas_call(
        functools.partial(_mem_attn_kernel, heads=heads, hd=dq // heads),
        out_shape=jax.ShapeDtypeStruct((b, n, dq), BF16),
        grid=(b, n // tm),
        in_specs=[pl.BlockSpec((1, tm, dq), lambda bb, i: (bb, i, 0)),
                  pl.BlockSpec((1, nm, dq), lambda bb, i: (bb, 0, 0)),
                  pl.BlockSpec((1, nm, dq), lambda bb, i: (bb, 0, 0))],
        out_specs=pl.BlockSpec((1, tm, dq), lambda bb, i: (bb, i, 0)),
        compiler_params=_params(("parallel", "parallel"), est),
        name="mem_attend",
    )(qm, mk, mv)


def _logf_kernel(h_ref, w_ref, b_ref, o_ref):
    z = jnp.dot(h_ref[...], w_ref[...], preferred_element_type=F32) + b_ref[...]
    o_ref[...] = jnp.minimum(z, 0.0) - jnp.log1p(jnp.exp(-jnp.abs(z)))


def log_forget(hk, w_f, b_f):
    m, k = hk.shape
    nh = w_f.shape[1]
    wp = jnp.pad(w_f, ((0, 0), (0, LANE - nh))).astype(BF16)
    bp = jnp.pad(b_f, (0, LANE - nh)).reshape(1, LANE).astype(F32)
    tm = _tile(m, MM_TILE)
    out = pl.pallas_call(
        _logf_kernel,
        out_shape=jax.ShapeDtypeStruct((m, LANE), F32),
        grid=(m // tm,),
        in_specs=[pl.BlockSpec((tm, k), lambda i: (i, 0)),
                  pl.BlockSpec((k, LANE), lambda i: (0, 0)),
                  pl.BlockSpec((1, LANE), lambda i: (0, 0))],
        out_specs=pl.BlockSpec((tm, LANE), lambda i: (i, 0)),
        compiler_params=_params(("parallel",), 2 * (tm * k * 2 + k * LANE * 2 + tm * LANE * 4)),
        name="log_forget",
    )(hk, wp, bp)
    return out[:, :nh]


def _cumsum_kernel(x_ref, o_ref, carry_ref, *, tb):
    @pl.when(pl.program_id(1) == 0)
    def _():
        carry_ref[...] = jnp.zeros_like(carry_ref)

    x = x_ref[0]
    row = lax.broadcasted_iota(jnp.int32, x.shape, 0)
    d = 1
    while d < tb:
        x = x + jnp.where(row >= d, pltpu.roll(x, d, 0), 0.0)
        d *= 2
    x = x + carry_ref[0:1, :]
    o_ref[0] = x
    carry_ref[...] = jnp.broadcast_to(x[tb - 1:tb, :], carry_ref.shape)


def cumsum_time(logf):
    b, t, nh = logf.shape
    tb = CUMSUM_TILE
    tp = -(-t // tb) * tb
    x = jnp.pad(logf, ((0, 0), (0, tp - t), (0, LANE - nh)))
    out = pl.pallas_call(
        functools.partial(_cumsum_kernel, tb=tb),
        out_shape=jax.ShapeDtypeStruct((b, tp, LANE), F32),
        grid=(b, tp // tb),
        in_specs=[pl.BlockSpec((1, tb, LANE), lambda bb, i: (bb, i, 0))],
        out_specs=pl.BlockSpec((1, tb, LANE), lambda bb, i: (bb, i, 0)),
        scratch_shapes=[pltpu.VMEM((8, LANE), F32)],
        compiler_params=_params(("parallel", "arbitrary"), 8 * tb * LANE * 4),
        name="cumsum_time",
    )(x)
    return out[:, :t, :nh]


LOG2E = 1.4426950408889634
_NT = (((1,), (1,)), ((), ()))


def _lanes(x, reps):
    return x if reps == 1 else jnp.concatenate([x] * reps, axis=1)


def _fox_prompt_kernel(q_ref, k_ref, v_ref, f_ref, o_ref, s0_ref, s1_ref, m_ref, acc_ref,
                       *, tq, tk, hd, nhb):
    qi = pl.program_id(1)
    m_ref[...] = jnp.full_like(m_ref, MASK_VALUE)
    acc_ref[...] = jnp.zeros_like(acc_ref)
    ones = jnp.ones((tk, hd), BF16)

    def logits(kb, s_ref, r0=0):
        off = pl.multiple_of(kb * tk, tk)
        for hh in range(nhb):
            sl = slice(hh * hd, (hh + 1) * hd)
            s = lax.dot_general(q_ref[r0:, sl], k_ref[pl.ds(off, tk), sl], _NT,
                                preferred_element_type=F32)
            s_ref[hh, r0:, :] = s - f_ref[hh, kb] * LOG2E

    def attend(kb, s_ref, diag=None):
        off = pl.multiple_of(kb * tk, tk)
        r0 = 0 if diag is None else diag * tk
        for hh in range(nhb):
            sl = slice(hh * hd, (hh + 1) * hd)
            s = s_ref[hh, r0:, :]
            if diag is not None:
                row = lax.broadcasted_iota(jnp.int32, s.shape, 0)
                col = lax.broadcasted_iota(jnp.int32, s.shape, 1)
                s = jnp.where(col <= row, s, MASK_VALUE)
            m_prev = m_ref[hh, r0:, :]
            m_new = jnp.maximum(m_prev, jnp.max(s, axis=-1, keepdims=True))
            alpha = jnp.exp2(m_prev - m_new)
            p = jnp.exp2(s - _lanes(m_new, tk // LANE))
            v = v_ref[pl.ds(off, tk), sl]
            pv = jnp.dot(p.astype(BF16), jnp.concatenate([v, ones], axis=1),
                         preferred_element_type=F32)
            acc_ref[hh, r0:, :] = _lanes(alpha, 2 * hd // LANE) * acc_ref[hh, r0:, :] + pv
            m_ref[hh, r0:, :] = m_new

    def body(jj, carry):
        kb = 2 * jj
        logits(kb + 1, s1_ref)
        attend(kb, s0_ref)
        logits(kb + 2, s0_ref)
        attend(kb + 1, s1_ref)
        return carry

    logits(0, s0_ref)
    lax.fori_loop(0, qi, body, 0)
    logits(2 * qi + 1, s1_ref, r0=tk)
    attend(2 * qi, s0_ref, diag=0)
    attend(2 * qi + 1, s1_ref, diag=1)

    for hh in range(nhb):
        acc = acc_ref[hh]
        o_ref[:, hh * hd:(hh + 1) * hd] = (acc[:, :hd] / acc[:, hd:]).astype(o_ref.dtype)


def fox_prompt(q, k, v, f_cum, hd):
    n, dm = q.shape
    nh = dm // hd
    nhb = 2 if nh % 2 == 0 else 1
    tq = _tile(n, FOX_TILE_Q)
    tk = tq // 2
    assert hd == LANE and tk % LANE == 0
    f_t = f_cum.T.reshape(nh, n // tk, 1, tk)
    wb = nhb * hd
    est = 2 * (2 * tq * wb * 2 + 2 * n * wb * 2 + nhb * (n // tk) * 8 * tk * 4) \
        + nhb * (tq * LANE * 4 + tq * 2 * hd * 4 + 5 * tq * tk * 4)
    return pl.pallas_call(
        functools.partial(_fox_prompt_kernel, tq=tq, tk=tk, hd=hd, nhb=nhb),
        out_shape=jax.ShapeDtypeStruct((n, dm), BF16),
        grid=(nh // nhb, n // tq),
        in_specs=[pl.BlockSpec((tq, wb), lambda h, i: (i, h)),
                  pl.BlockSpec((n, wb), lambda h, i: (0, h)),
                  pl.BlockSpec((n, wb), lambda h, i: (0, h)),
                  pl.BlockSpec((nhb, n // tk, 1, tk), lambda h, i: (h, 0, 0, 0))],
        out_specs=pl.BlockSpec((tq, wb), lambda h, i: (i, h)),
        scratch_shapes=[pltpu.VMEM((nhb, tq, tk), F32), pltpu.VMEM((nhb, tq, tk), F32),
                        pltpu.VMEM((nhb, tq, LANE), F32), pltpu.VMEM((nhb, tq, 2 * hd), F32)],
        compiler_params=_params(("parallel", "parallel"), est),
        name="fox_prompt",
    )(q, k, v, f_t)


def _fox_cached_kernel(q_ref, kp_ref, vp_ref, kn_ref, vn_ref, fp_ref, fn_ref, rh_ref, rt_ref,
                       ch_ref, ct_ref, o_ref, m_ref, acc_ref, *, sub, hd):
    j = pl.program_id(1)
    q = q_ref[0]
    rh = rh_ref[...]

    @pl.when(j == 0)
    def _():
        m_ref[...] = jnp.full_like(m_ref, MASK_VALUE)
        acc_ref[...] = jnp.zeros_like(acc_ref)

    def logits(k, f, valid):
        s = lax.dot_general(q, k, _NT, preferred_element_type=F32) - f * LOG2E
        return jnp.where(valid, s, MASK_VALUE)

    def absorb(s, v):
        width = s.shape[1]
        m_prev = m_ref[...]
        m_new = jnp.maximum(m_prev, jnp.max(s, axis=-1, keepdims=True))
        alpha = jnp.exp2(m_prev - m_new)
        p = jnp.exp2(s - (_lanes(m_new, width // LANE) if width % LANE == 0 else m_new[:, :1]))
        vaug = jnp.concatenate([v, jnp.ones_like(v)], axis=1)
        pv = jnp.dot(p.astype(BF16), vaug, preferred_element_type=F32)
        acc_ref[...] = _lanes(alpha, 2 * hd // LANE) * acc_ref[...] + pv
        m_ref[...] = m_new

    subs = [slice(c0, c0 + sub) for c0 in range(0, kp_ref.shape[1], sub)]
    s_all = [logits(kp_ref[0, cs, :].astype(BF16), fp_ref[0, :, cs], rh == ch_ref[:, cs])
             for cs in subs]
    for s, cs in zip(s_all, subs):
        absorb(s, vp_ref[0, cs, :].astype(BF16))

    @pl.when(j == pl.num_programs(1) - 1)
    def _():
        r = kn_ref.shape[1]
        absorb(logits(kn_ref[0], fn_ref[0], (rh == ch_ref[:, :r]) & (ct_ref[...] <= rt_ref[...])),
               vn_ref[0])
        acc = acc_ref[...]
        o_ref[0] = (acc[:, :hd] / acc[:, hd:]).astype(o_ref.dtype)


def fox_cached(q, k_past, v_past, k_new, v_new, f_all, hd):
    b, n, dm = q.shape
    p, nh = k_past.shape[1:3]
    r = n * nh
    tp = _tile(p, FOX_PAST_TILE)
    c = tp * nh
    sub = c // 2 if c % (2 * LANE * nh) == 0 else c
    assert c % LANE == 0 and c >= r
    heads = jnp.arange(c, dtype=jnp.int32) % nh
    times = jnp.arange(r, dtype=jnp.int32) // nh
    new = pl.BlockSpec((1, r, hd), lambda bb, j: (bb, 0, 0))
    past = pl.BlockSpec((1, c, hd), lambda bb, j: (bb, j, 0))
    whole = lambda shape: pl.BlockSpec(shape, lambda bb, j: (0, 0))
    est = 2 * (2 * c * hd * 4 + 4 * r * hd * 2) + 6 * r * sub * 4 + 3 * r * 2 * hd * 4
    out = pl.pallas_call(
        functools.partial(_fox_cached_kernel, sub=sub, hd=hd),
        out_shape=jax.ShapeDtypeStruct((b, r, hd), BF16),
        grid=(b, p // tp),
        in_specs=[new, past, past, new, new,
                  pl.BlockSpec((1, 1, c), lambda bb, j: (bb, 0, j)),
                  pl.BlockSpec((1, 1, r), lambda bb, j: (bb, 0, 0)),
                  whole((r, 1)), whole((r, 1)), whole((1, c)), whole((1, r))],
        out_specs=new,
        scratch_shapes=[pltpu.VMEM((r, LANE), F32), pltpu.VMEM((r, 2 * hd), F32)],
        compiler_params=_params(("parallel", "arbitrary"), est),
        name="fox_cached",
    )(q.reshape(b, r, hd), k_past.reshape(b, p * nh, hd), v_past.reshape(b, p * nh, hd),
      k_new.reshape(b, r, hd), v_new.reshape(b, r, hd),
      f_all[:, :p].reshape(b, 1, p * nh), f_all[:, p:].reshape(b, 1, r),
      heads[:r].reshape(r, 1), times.reshape(r, 1), heads.reshape(1, c), times.reshape(1, r))
    return out.reshape(b, n, dm)


def _trunk(x, pool_prev, mem_k, mem_v, past, w):
    b, n, d = x.shape
    m = b * n
    depth, n_a = w["depth"], w["n_a"]
    c, hd, heads = w["d_pool"], w["hd_fox"], w["mem_heads"]
    assert n >= POOL_STATE
    pos0 = 0 if past is None else past[0].shape[1]
    x2 = x.reshape(m, d)
    h = rms_cast(x2, w["norm_g"][0, 0])
    new_pool = []
    k_new = v_new = logf_new = None
    k_bf = v_bf = f_all = None
    for l in range(depth):
        g = w["norm_g"][l]
        a = ffn_up(h, w["wg"][l][0], w["wu"][l][0])
        x2, (h,) = mm_resid([a], w["wd"][l][0], x2, g[1], 0.5, [g[2]])
        if l < n_a:
            w_in, w_out = w["w_in_a"][l], w["w_out_a"][l]
            (u,) = mm(h, w_in, [F32], cols=(0, c))
            u3 = u.reshape(b, n, c)
            main = pool_mixer(u3, pool_prev[l], pos0, w["pool_w"][l], w["pool_scale"][l])
            new_pool.append(u3[:, n - POOL_STATE:, :])
        else:
            j = l - n_a
            w_in, w_out = w["w_in_b"][j], w["w_out_b"][j]
            (q,) = mm(h, w_in, [BF16], scale=LOG2E * hd ** -0.5, cols=(0, c))
            if past is None:
                main = fox_prompt(q, k_bf, v_bf, f_all[0], hd).reshape(b, n, c)
            else:
                main = fox_cached(q.reshape(b, n, c), past[0], past[1], k_bf.reshape(b, n, c),
                                  v_bf.reshape(b, n, c), f_all, hd)
        (qm,) = mm(h, w_in, [BF16], cols=(c, d))
        mem = mem_attend(qm.reshape(b, n, d - c), mem_k[l], mem_v[l], heads)
        x2, (h,) = mm_resid([main.reshape(m, c), mem.reshape(m, d - c)], w_out, x2, g[3], 1.0, [g[4]])
        a = ffn_up(h, w["wg"][l][1], w["wu"][l][1])
        g_next = []
        if l + 1 < depth:
            g_next.append(w["norm_g"][l + 1, 0])
        if l == n_a - 1:
            g_next.append(w["g_kv"])
        x2, hs = mm_resid([a], w["wd"][l][1], x2, g[5], 0.5, g_next)
        if l + 1 < depth:
            h = hs[0]
        if l == n_a - 1:
            hk = hs[-1]
            k_new, k_bf = mm(hk, w["w_kv"], [F32, BF16], cols=(0, c))
            v_new, v_bf = mm(hk, w["w_kv"], [F32, BF16], cols=(c, 2 * c))
            logf_new = log_forget(hk, w["w_f"], w["b_f"]).reshape(b, n, -1)
            logf_all = logf_new if past is None else jnp.concatenate(
                [past[2].astype(F32), logf_new], axis=1)
            f_all = cumsum_time(logf_all)
    nh = c // hd
    return (x2.reshape(b, n, d), jnp.stack(new_pool), k_new.reshape(b, n, nh, hd),
            v_new.reshape(b, n, nh, hd), logf_new)


def _cast_pad_kernel(x_ref, o_ref, *, rows, cols):
    tr, tc = o_ref.shape
    r = pl.program_id(0) * tr + lax.broadcasted_iota(jnp.int32, (tr, tc), 0)
    c = pl.program_id(1) * tc + lax.broadcasted_iota(jnp.int32, (tr, tc), 1)
    o_ref[...] = jnp.where((r < rows) & (c < cols), x_ref[...], 0.0).astype(o_ref.dtype)


def _pad_ff(w4, l, s, axis):
    rows, cols = w4.shape[2:]
    shape = [rows, cols]
    shape[axis] += -shape[axis] % FF_ALIGN
    tr = _tile(shape[0], CAST_TILE_ELEMS // MM_TILE)
    tc = _tile(shape[1], CAST_TILE_ELEMS // tr)
    return pl.pallas_call(
        functools.partial(_cast_pad_kernel, rows=rows, cols=cols),
        out_shape=jax.ShapeDtypeStruct(tuple(shape), BF16),
        grid=(shape[0] // tr, shape[1] // tc),
        in_specs=[pl.BlockSpec((None, None, tr, tc), lambda i, j: (l, s, i, j))],
        out_specs=pl.BlockSpec((tr, tc), lambda i, j: (i, j)),
        compiler_params=_params(("parallel", "parallel"), 2 * tr * tc * 6 + 3 * tr * tc * 4),
        name="cast_pad",
    )(w4)


def kernel(x_prompt, x_sample, mem_prompt, cache_fox_k, cache_fox_v, cache_fox_logf, cache_mem_k, cache_mem_v, state_pool, norm_g, w_ffn_gate, w_ffn_up, w_ffn_down, w_in_a, pool_w, pool_scale, w_out_a, w_in_b, w_out_b, g_kv, w_kv, w_f, b_f, g_mem, w_mem_kv):
    depth = norm_g.shape[0]
    n_a, db, _, c = state_pool.shape
    hd = cache_fox_k.shape[3]
    _, _, n_mem, heads, mem_hd = cache_mem_k.shape
    dq = heads * mem_hd
    bp, _, d = x_prompt.shape
    w = dict(
        depth=depth, n_a=n_a, d_pool=c, hd_fox=hd, mem_heads=heads,
        norm_g=norm_g, g_kv=g_kv, w_f=w_f, b_f=b_f, pool_scale=pool_scale,
        wg=[[_pad_ff(w_ffn_gate, l, s, 1) for s in range(2)] for l in range(depth)],
        wu=[[_pad_ff(w_ffn_up, l, s, 1) for s in range(2)] for l in range(depth)],
        wd=[[_pad_ff(w_ffn_down, l, s, 0) for s in range(2)] for l in range(depth)],
        w_in_a=w_in_a.astype(BF16), w_out_a=w_out_a.astype(BF16),
        w_in_b=w_in_b.astype(BF16), w_out_b=w_out_b.astype(BF16),
        pool_w=pool_w.astype(BF16), w_kv=w_kv.astype(BF16),
    )
    mem2 = mem_prompt.reshape(bp * n_mem, d)
    mem_kv = [mm(rms_cast(mem2, g_mem[l]), w_mem_kv[l].astype(BF16), [F32])[0] for l in range(depth)]
    mem_k_p = jnp.stack([kv[:, :dq].reshape(bp, n_mem, dq) for kv in mem_kv])
    mem_v_p = jnp.stack([kv[:, dq:].reshape(bp, n_mem, dq) for kv in mem_kv])

    zero_prev = jnp.zeros((n_a, bp, HALO, c), F32)
    y_p, pool_p, k_p, v_p, logf_p = _trunk(x_prompt, zero_prev, mem_k_p, mem_v_p, None, w)

    prev_s = jnp.pad(state_pool.astype(F32), ((0, 0), (0, 0), (1, 0), (0, 0)))
    past = (cache_fox_k, cache_fox_v, cache_fox_logf)
    y_s, pool_s, k_s, v_s, logf_s = _trunk(
        x_sample, prev_s, cache_mem_k.reshape(depth, db, n_mem, dq),
        cache_mem_v.reshape(depth, db, n_mem, dq), past, w)

    return (y_p, y_s, k_p, v_p, logf_p,
            mem_k_p.reshape(depth, bp, n_mem, heads, mem_hd),
            mem_v_p.reshape(depth, bp, n_mem, heads, mem_hd),
            pool_p, k_s, v_s, logf_s, pool_s)
```

```python
import functools

import jax
import jax.numpy as jnp
from jax import lax
from jax.experimental import pallas as pl
from jax.experimental.pallas import tpu as pltpu

EPS = 1e-6
POOL_WINDOWS = (2, 4, 8, 16)
POOL_STATE = max(POOL_WINDOWS) - 1
HALO = POOL_STATE + 1
LANE = 128
BF16_ROWS = 16
MASK_VALUE = -1e30
VMEM_CAP = 60 * 1024 * 1024
VMEM_FLOOR = 32 * 1024 * 1024
VMEM_SLACK = 4 * 1024 * 1024
F32 = jnp.float32
BF16 = jnp.bfloat16

MM_TILE = 1024
FFN_UP_TILE_N = 512
FF_ALIGN = 1024
RESID_TILE_M, RESID_TILE_K = 512, 1024
RESID_MAX_CHUNKS = 8
ROW_TILE = 512
CAST_TILE_ELEMS = 2 * 1024 * 1024
POOL_TILE = 512
FOX_TILE_Q = 1024
FOX_PAST_TILE = 256
CUMSUM_TILE = 256


def _tile(n, pref):
    if n <= pref:
        return n
    t = pref
    while n % t:
        t //= 2
    return t


def _params(sem, est_bytes):
    limit = int(min(VMEM_CAP, max(VMEM_FLOOR, est_bytes * 5 // 4 + VMEM_SLACK)))
    return pltpu.CompilerParams(dimension_semantics=sem, vmem_limit_bytes=limit)


def _rms_normalize(xf):
    return xf * lax.rsqrt(jnp.mean(xf * xf, axis=-1, keepdims=True) + EPS)


def _rms_cast_kernel(x_ref, g_ref, o_ref):
    o_ref[...] = (_rms_normalize(x_ref[...]) * g_ref[...]).astype(o_ref.dtype)


def rms_cast(x, g):
    m, d = x.shape
    tm = _tile(m, ROW_TILE)
    return pl.pallas_call(
        _rms_cast_kernel,
        out_shape=jax.ShapeDtypeStruct((m, d), BF16),
        grid=(m // tm,),
        in_specs=[pl.BlockSpec((tm, d), lambda i: (i, 0)),
                  pl.BlockSpec((1, d), lambda i: (0, 0))],
        out_specs=pl.BlockSpec((tm, d), lambda i: (i, 0)),
        compiler_params=_params(("parallel",), 2 * tm * d * 6),
        name="rms_cast",
    )(x, g.reshape(1, d).astype(F32))


def _mm_kernel(h_ref, w_ref, *o_refs, scale):
    acc = jnp.dot(h_ref[...], w_ref[...], preferred_element_type=F32)
    if scale is not None:
        acc = acc * scale
    for o_ref in o_refs:
        o_ref[...] = acc.astype(o_ref.dtype)


def mm(h, w, out_dtypes, scale=None, cols=None):
    m, k = h.shape
    c0, c1 = cols if cols is not None else (0, w.shape[1])
    n = c1 - c0
    tm, tn = _tile(m, MM_TILE), _tile(n, MM_TILE)
    assert c0 % tn == 0
    j0 = c0 // tn
    est = 2 * (tm * k * 2 + k * tn * 2) + tm * tn * 4 + sum(
        2 * tm * tn * jnp.dtype(t).itemsize for t in out_dtypes)
    outs = pl.pallas_call(
        functools.partial(_mm_kernel, scale=scale),
        out_shape=[jax.ShapeDtypeStruct((m, n), t) for t in out_dtypes],
        grid=(m // tm, n // tn),
        in_specs=[pl.BlockSpec((tm, k), lambda i, j: (i, 0)),
                  pl.BlockSpec((k, tn), lambda i, j: (0, j + j0))],
        out_specs=[pl.BlockSpec((tm, tn), lambda i, j: (i, j)) for _ in out_dtypes],
        compiler_params=_params(("parallel", "parallel"), est),
        name="mm",
    )(h, w)
    return outs


def _ffn_up_kernel(h_ref, wg_ref, wu_ref, o_ref, *, full_blocks, part):
    j = pl.program_id(1)
    tn = o_ref.shape[1]

    def swiglu(width):
        h = h_ref[...]
        g = jnp.dot(h, wg_ref[:, :width], preferred_element_type=F32)
        u = jnp.dot(h, wu_ref[:, :width], preferred_element_type=F32)
        o_ref[:, :width] = (g * jax.nn.sigmoid(g) * u).astype(o_ref.dtype)
        if width < tn:
            o_ref[:, width:] = jnp.zeros((o_ref.shape[0], tn - width), o_ref.dtype)

    @pl.when(j < full_blocks)
    def _():
        swiglu(tn)

    if part:
        @pl.when(j == full_blocks)
        def _():
            swiglu(part)

    @pl.when(j >= full_blocks + (1 if part else 0))
    def _():
        o_ref[...] = jnp.zeros_like(o_ref)


def ffn_up(h, wg, wu, n_real):
    m, k = h.shape
    n = wg.shape[1]
    tm, tn = _tile(m, MM_TILE), _tile(n, FFN_UP_TILE_N)
    part = -(-(n_real % tn) // LANE) * LANE
    est = 2 * (tm * k * 2 + 2 * k * tn * 2 + tm * tn * 2) + 3 * tm * tn * 4
    return pl.pallas_call(
        functools.partial(_ffn_up_kernel, full_blocks=n_real // tn, part=part),
        out_shape=jax.ShapeDtypeStruct((m, n), BF16),
        grid=(m // tm, n // tn),
        in_specs=[pl.BlockSpec((tm, k), lambda i, j: (i, 0)),
                  pl.BlockSpec((k, tn), lambda i, j: (0, j)),
                  pl.BlockSpec((k, tn), lambda i, j: (0, j))],
        out_specs=pl.BlockSpec((tm, tn), lambda i, j: (i, j)),
        compiler_params=_params(("parallel", "parallel"), est),
        name="ffn_up",
    )(h, wg, wu)


def _mm_resid_kernel(*refs, kblocks, coef, n_next, ntiles, nchunk):
    n_lhs = len(kblocks)
    a_refs = refs[:n_lhs]
    w_ref, x_ref, gp_ref = refs[n_lhs:n_lhs + 3]
    gn_refs = refs[n_lhs + 3:n_lhs + 3 + n_next]
    xo_ref = refs[n_lhs + 3 + n_next]
    h_refs = refs[n_lhs + 4 + n_next:-2]
    acc_refs = refs[-2:]
    i, k = pl.program_id(0), pl.program_id(1)
    rows = xo_ref.shape[0]

    @pl.when((i == 0) & (k == 0))
    def _():
        for acc_ref in acc_refs:
            acc_ref[...] = jnp.zeros_like(acc_ref)

    def accumulate(acc_ref):
        a = a_refs[0][...]
        start = kblocks[0]
        for a_ref, nb in zip(a_refs[1:], kblocks[1:]):
            a = jnp.where(k >= start, a_ref[...], a)
            start += nb
        prev = jnp.where(k == 0, 0.0, acc_ref[...])
        acc_ref[...] = prev + jnp.dot(a, w_ref[...], preferred_element_type=F32)

    def epilogue(acc_ref):
        r0 = pl.multiple_of(jnp.minimum(k, nchunk - 1) * rows, rows)
        y = acc_ref[pl.ds(r0, rows), :]
        xn = x_ref[...] + coef * (_rms_normalize(y) * gp_ref[...])
        xo_ref[...] = xn
        if n_next:
            r = _rms_normalize(xn)
            for gn_ref, h_ref in zip(gn_refs, h_refs):
                h_ref[...] = (r * gn_ref[...]).astype(h_ref.dtype)

    for parity in range(2):
        cur, prev = acc_refs[parity], acc_refs[1 - parity]

        @pl.when((i % 2 == parity) & (i < ntiles))
        def _(cur=cur, prev=prev):
            accumulate(cur)
            epilogue(prev)

    @pl.when(i == ntiles)
    def _():
        epilogue(acc_refs[1 - ntiles % 2])


def mm_resid(lhs, w, x, g_post, coef, g_next):
    m, d = x.shape
    widths = [a.shape[1] for a in lhs]
    assert sum(widths) == w.shape[0] and w.shape[1] == d
    tk = RESID_TILE_K
    while any(kw % tk for kw in widths):
        tk //= 2
    tm = _tile(m, RESID_TILE_M)
    ntiles = m // tm
    kblocks = tuple(kw // tk for kw in widths)
    starts = [sum(kblocks[:j]) for j in range(len(lhs))]
    n_next = len(g_next)
    nk = sum(kblocks)
    nchunk = 1
    while nchunk * 2 <= min(nk, RESID_MAX_CHUNKS) and tm % (nchunk * 2 * BF16_ROWS) == 0:
        nchunk *= 2
    rows = tm // nchunk
    gain = pl.BlockSpec((1, d), lambda i, k: (0, 0))

    def step(i, k):
        return jnp.where(i == ntiles, nk - 1, k)

    def lhs_spec(start, nb):
        return pl.BlockSpec(
            (tm, tk),
            lambda i, k: (jnp.minimum(i, ntiles - 1), jnp.clip(step(i, k) - start, 0, nb - 1)))

    chunk = pl.BlockSpec(
        (rows, d), lambda i, k: (jnp.maximum((i - 1) * nchunk + jnp.minimum(k, nchunk - 1), 0), 0))
    est = len(lhs) * 2 * tm * tk * 2 + 2 * tk * d * 2 + 2 * tm * d * 4 \
        + 2 * rows * d * (8 + 2 * n_next) + 8 * rows * d * 4 + tm * tk * 2
    outs = pl.pallas_call(
        functools.partial(_mm_resid_kernel, kblocks=kblocks, coef=coef, n_next=n_next,
                          ntiles=ntiles, nchunk=nchunk),
        out_shape=[jax.ShapeDtypeStruct((m, d), F32)]
        + [jax.ShapeDtypeStruct((m, d), BF16) for _ in range(n_next)],
        grid=(ntiles + 1, nk),
        in_specs=[lhs_spec(s, nb) for s, nb in zip(starts, kblocks)]
        + [pl.BlockSpec((tk, d), lambda i, k: (step(i, k), 0)), chunk, gain] + [gain] * n_next,
        out_specs=[chunk] * (1 + n_next),
        scratch_shapes=[pltpu.VMEM((tm, d), F32), pltpu.VMEM((tm, d), F32)],
        compiler_params=_params(("arbitrary", "arbitrary"), est),
        name="mm_resid",
    )(*lhs, w, x, g_post.reshape(1, d).astype(F32), *[g.reshape(1, d).astype(F32) for g in g_next])
    return outs[0], outs[1:]


def _pool_kernel(u_ref, halo_ref, prev_ref, w_ref, sc_ref, o_ref, ext_ref, *, tm, pos0, group):
    i = pl.program_id(1)

    @pl.when(i == 0)
    def _():
        ext_ref[0:HALO, :] = prev_ref[0]

    @pl.when(i > 0)
    def _():
        ext_ref[0:HALO, :] = halo_ref[0]

    ext_ref[HALO:HALO + tm, :] = u_ref[0]
    pos = pos0 + i * tm + lax.broadcasted_iota(jnp.int32, (tm, 1), 0)
    for gi, win in enumerate(POOL_WINDOWS):
        sl = slice(gi * group, (gi + 1) * group)
        ug = ext_ref[HALO:HALO + tm, sl]
        wsum = ug
        for j in range(1, win):
            wsum = wsum + ext_ref[HALO - j:HALO - j + tm, sl]
        cnt = jnp.minimum(pos + 1, win).astype(F32)
        diff = wsum / cnt - ug
        y = jnp.dot(diff.astype(BF16), w_ref[gi], preferred_element_type=F32)
        o_ref[0, :, sl] = (y * sc_ref[:, sl]).astype(o_ref.dtype)


def pool_mixer(u, prev, pos0, pool_w, pool_scale):
    b, n, c = u.shape
    group = c // len(POOL_WINDOWS)
    tm = _tile(n, POOL_TILE)
    steps = tm // HALO
    est = 2 * (tm * c * 4 + 2 * HALO * c * 4 + pool_w.size * 2 + tm * c * 2) + (tm + HALO) * c * 4 \
        + 4 * tm * group * 4
    return pl.pallas_call(
        functools.partial(_pool_kernel, tm=tm, pos0=pos0, group=group),
        out_shape=jax.ShapeDtypeStruct((b, n, c), BF16),
        grid=(b, n // tm),
        in_specs=[pl.BlockSpec((1, tm, c), lambda bb, i: (bb, i, 0)),
                  pl.BlockSpec((1, HALO, c), lambda bb, i: (bb, jnp.maximum(i * steps - 1, 0), 0)),
                  pl.BlockSpec((1, HALO, c), lambda bb, i: (bb, 0, 0)),
                  pl.BlockSpec(pool_w.shape, lambda bb, i: (0, 0, 0)),
                  pl.BlockSpec((1, c), lambda bb, i: (0, 0))],
        out_specs=pl.BlockSpec((1, tm, c), lambda bb, i: (bb, i, 0)),
        scratch_shapes=[pltpu.VMEM((tm + HALO, c), F32)],
        compiler_params=_params(("parallel", "parallel"), est),
        name="pool_mixer",
    )(u, u, prev, pool_w, pool_scale.reshape(1, c).astype(F32))


def _mem_attn_kernel(q_ref, k_ref, v_ref, o_ref, *, heads, hd):
    scale = hd ** -0.5
    for hh in range(heads):
        sl = slice(hh * hd, (hh + 1) * hd)
        q = q_ref[0, :, sl]
        k = k_ref[0, :, sl].astype(BF16)
        v = v_ref[0, :, sl].astype(BF16)
        s = lax.dot_general(q, k, (((1,), (1,)), ((), ())), preferred_element_type=F32) * scale
        p = jnp.exp(s - jnp.max(s, axis=-1, keepdims=True))
        p = p / jnp.sum(p, axis=-1, keepdims=True)
        o_ref[0, :, sl] = jnp.dot(p.astype(BF16), v, preferred_element_type=F32).astype(o_ref.dtype)


def mem_attend(qm, mk, mv, heads):
    b, n, dq = qm.shape
    nm = mk.shape[1]
    tm = _tile(n, ROW_TILE)
    est = 2 * (2 * tm * dq * 2 + 2 * nm * dq * 4) + 4 * tm * nm * 4
    return pl.pallas_call(
        functools.partial(_mem_attn_kernel, heads=heads, hd=dq // heads),
        out_shape=jax.ShapeDtypeStruct((b, n, dq), BF16),
        grid=(b, n // tm),
        in_specs=[pl.BlockSpec((1, tm, dq), lambda bb, i: (bb, i, 0)),
                  pl.BlockSpec((1, nm, dq), lambda bb, i: (bb, 0, 0)),
                  pl.BlockSpec((1, nm, dq), lambda bb, i: (bb, 0, 0))],
        out_specs=pl.BlockSpec((1, tm, dq), lambda bb, i: (bb, i, 0)),
        compiler_params=_params(("parallel", "parallel"), est),
        name="mem_attend",
    )(qm, mk, mv)


def _logf_kernel(h_ref, w_ref, b_ref, o_ref):
    z = jnp.dot(h_ref[...], w_ref[...], preferred_element_type=F32) + b_ref[...]
    o_ref[...] = jnp.minimum(z, 0.0) - jnp.log1p(jnp.exp(-jnp.abs(z)))


def log_forget(hk, w_f, b_f):
    m, k = hk.shape
    nh = w_f.shape[1]
    wp = jnp.pad(w_f, ((0, 0), (0, LANE - nh))).astype(BF16)
    bp = jnp.pad(b_f, (0, LANE - nh)).reshape(1, LANE).astype(F32)
    tm = _tile(m, MM_TILE)
    out = pl.pallas_call(
        _logf_kernel,
        out_shape=jax.ShapeDtypeStruct((m, LANE), F32),
        grid=(m // tm,),
        in_specs=[pl.BlockSpec((tm, k), lambda i: (i, 0)),
                  pl.BlockSpec((k, LANE), lambda i: (0, 0)),
                  pl.BlockSpec((1, LANE), lambda i: (0, 0))],
        out_specs=pl.BlockSpec((tm, LANE), lambda i: (i, 0)),
        compiler_params=_params(("parallel",), 2 * (tm * k * 2 + k * LANE * 2 + tm * LANE * 4)),
        name="log_forget",
    )(hk, wp, bp)
    return out[:, :nh]


def _cumsum_kernel(x_ref, o_ref, carry_ref, *, tb):
    @pl.when(pl.program_id(1) == 0)
    def _():
        carry_ref[...] = jnp.zeros_like(carry_ref)

    x = x_ref[0]
    row = lax.broadcasted_iota(jnp.int32, x.shape, 0)
    d = 1
    while d < tb:
        x = x + jnp.where(row >= d, pltpu.roll(x, d, 0), 0.0)
        d *= 2
    x = x + carry_ref[0:1, :]
    o_ref[0] = x
    carry_ref[...] = jnp.broadcast_to(x[tb - 1:tb, :], carry_ref.shape)


def cumsum_time(logf):
    b, t, nh = logf.shape
    tb = CUMSUM_TILE
    tp = -(-t // tb) * tb
    x = jnp.pad(logf, ((0, 0), (0, tp - t), (0, LANE - nh)))
    out = pl.pallas_call(
        functools.partial(_cumsum_kernel, tb=tb),
        out_shape=jax.ShapeDtypeStruct((b, tp, LANE), F32),
        grid=(b, tp // tb),
        in_specs=[pl.BlockSpec((1, tb, LANE), lambda bb, i: (bb, i, 0))],
        out_specs=pl.BlockSpec((1, tb, LANE), lambda bb, i: (bb, i, 0)),
        scratch_shapes=[pltpu.VMEM((8, LANE), F32)],
        compiler_params=_params(("parallel", "arbitrary"), 8 * tb * LANE * 4),
        name="cumsum_time",
    )(x)
    return out[:, :t, :nh]


LOG2E = 1.4426950408889634
_NT = (((1,), (1,)), ((), ()))


def _lanes(x, reps):
    return x if reps == 1 else jnp.concatenate([x] * reps, axis=1)


def _fox_prompt_kernel(q_ref, k_ref, v_ref, f_ref, o_ref, s0_ref, s1_ref, m_ref, acc_ref,
                       *, tq, tk, hd, nhb):
    qi = pl.program_id(1)
    m_ref[...] = jnp.full_like(m_ref, MASK_VALUE)
    acc_ref[...] = jnp.zeros_like(acc_ref)
    ones = jnp.ones((tk, hd), BF16)

    def logits(kb, s_ref, r0=0):
        off = pl.multiple_of(kb * tk, tk)
        for hh in range(nhb):
            sl = slice(hh * hd, (hh + 1) * hd)
            s = lax.dot_general(q_ref[r0:, sl], k_ref[pl.ds(off, tk), sl], _NT,
                                preferred_element_type=F32)
            s_ref[hh, r0:, :] = s - f_ref[hh, kb] * LOG2E

    def attend(kb, s_ref, diag=None):
        off = pl.multiple_of(kb * tk, tk)
        r0 = 0 if diag is None else diag * tk
        for hh in range(nhb):
            sl = slice(hh * hd, (hh + 1) * hd)
            s = s_ref[hh, r0:, :]
            if diag is not None:
                row = lax.broadcasted_iota(jnp.int32, s.shape, 0)
                col = lax.broadcasted_iota(jnp.int32, s.shape, 1)
                s = jnp.where(col <= row, s, MASK_VALUE)
            m_prev = m_ref[hh, r0:, :]
            m_new = jnp.maximum(m_prev, jnp.max(s, axis=-1, keepdims=True))
            alpha = jnp.exp2(m_prev - m_new)
            p = jnp.exp2(s - _lanes(m_new, tk // LANE))
            v = v_ref[pl.ds(off, tk), sl]
            pv = jnp.dot(p.astype(BF16), jnp.concatenate([v, ones], axis=1),
                         preferred_element_type=F32)
            acc_ref[hh, r0:, :] = _lanes(alpha, 2 * hd // LANE) * acc_ref[hh, r0:, :] + pv
            m_ref[hh, r0:, :] = m_new

    def body(jj, carry):
        kb = 2 * jj
        logits(kb + 1, s1_ref)
        attend(kb, s0_ref)
        logits(kb + 2, s0_ref)
        attend(kb + 1, s1_ref)
        return carry

    logits(0, s0_ref)
    lax.fori_loop(0, qi, body, 0)
    logits(2 * qi + 1, s1_ref, r0=tk)
    attend(2 * qi, s0_ref, diag=0)
    attend(2 * qi + 1, s1_ref, diag=1)

    for hh in range(nhb):
        acc = acc_ref[hh]
        o_ref[:, hh * hd:(hh + 1) * hd] = (acc[:, :hd] / acc[:, hd:]).astype(o_ref.dtype)


def fox_prompt(q, k, v, f_cum, hd):
    n, dm = q.shape
    nh = dm // hd
    nhb = 2 if nh % 2 == 0 else 1
    tq = _tile(n, FOX_TILE_Q)
    tk = tq // 2
    assert hd == LANE and tk % LANE == 0
    f_t = f_cum.T.reshape(nh, n // tk, 1, tk)
    wb = nhb * hd
    est = 2 * (2 * tq * wb * 2 + 2 * n * wb * 2 + nhb * (n // tk) * 8 * tk * 4) \
        + nhb * (tq * LANE * 4 + tq * 2 * hd * 4 + 5 * tq * tk * 4)
    return pl.pallas_call(
        functools.partial(_fox_prompt_kernel, tq=tq, tk=tk, hd=hd, nhb=nhb),
        out_shape=jax.ShapeDtypeStruct((n, dm), BF16),
        grid=(nh // nhb, n // tq),
        in_specs=[pl.BlockSpec((tq, wb), lambda h, i: (i, h)),
                  pl.BlockSpec((n, wb), lambda h, i: (0, h)),
                  pl.BlockSpec((n, wb), lambda h, i: (0, h)),
                  pl.BlockSpec((nhb, n // tk, 1, tk), lambda h, i: (h, 0, 0, 0))],
        out_specs=pl.BlockSpec((tq, wb), lambda h, i: (i, h)),
        scratch_shapes=[pltpu.VMEM((nhb, tq, tk), F32), pltpu.VMEM((nhb, tq, tk), F32),
                        pltpu.VMEM((nhb, tq, LANE), F32), pltpu.VMEM((nhb, tq, 2 * hd), F32)],
        compiler_params=_params(("parallel", "parallel"), est),
        name="fox_prompt",
    )(q, k, v, f_t)


def _fox_cached_kernel(q_ref, kp_ref, vp_ref, kn_ref, vn_ref, fp_ref, fn_ref, rh_ref, rt_ref,
                       ch_ref, ct_ref, o_ref, m_ref, acc_ref, *, sub, hd):
    j = pl.program_id(1)
    q = q_ref[0]
    rh = rh_ref[...]

    @pl.when(j == 0)
    def _():
        m_ref[...] = jnp.full_like(m_ref, MASK_VALUE)
        acc_ref[...] = jnp.zeros_like(acc_ref)

    def logits(k, f, valid):
        s = lax.dot_general(q, k, _NT, preferred_element_type=F32) - f * LOG2E
        return jnp.where(valid, s, MASK_VALUE)

    def absorb(s, v):
        width = s.shape[1]
        m_prev = m_ref[...]
        m_new = jnp.maximum(m_prev, jnp.max(s, axis=-1, keepdims=True))
        alpha = jnp.exp2(m_prev - m_new)
        p = jnp.exp2(s - (_lanes(m_new, width // LANE) if width % LANE == 0 else m_new[:, :1]))
        vaug = jnp.concatenate([v, jnp.ones_like(v)], axis=1)
        pv = jnp.dot(p.astype(BF16), vaug, preferred_element_type=F32)
        acc_ref[...] = _lanes(alpha, 2 * hd // LANE) * acc_ref[...] + pv
        m_ref[...] = m_new

    subs = [slice(c0, c0 + sub) for c0 in range(0, kp_ref.shape[1], sub)]
    s_all = [logits(kp_ref[0, cs, :].astype(BF16), fp_ref[0, :, cs], rh == ch_ref[:, cs])
             for cs in subs]
    for s, cs in zip(s_all, subs):
        absorb(s, vp_ref[0, cs, :].astype(BF16))

    @pl.when(j == pl.num_programs(1) - 1)
    def _():
        r = kn_ref.shape[1]
        absorb(logits(kn_ref[0], fn_ref[0], (rh == ch_ref[:, :r]) & (ct_ref[...] <= rt_ref[...])),
               vn_ref[0])
        acc = acc_ref[...]
        o_ref[0] = (acc[:, :hd] / acc[:, hd:]).astype(o_ref.dtype)


def fox_cached(q, k_past, v_past, k_new, v_new, f_all, hd):
    b, n, dm = q.shape
    p, nh = k_past.shape[1:3]
    r = n * nh
    tp = _tile(p, FOX_PAST_TILE)
    c = tp * nh
    sub = c // 2 if c % (2 * LANE * nh) == 0 else c
    assert c % LANE == 0 and c >= r
    heads = jnp.arange(c, dtype=jnp.int32) % nh
    times = jnp.arange(r, dtype=jnp.int32) // nh
    new = pl.BlockSpec((1, r, hd), lambda bb, j: (bb, 0, 0))
    past = pl.BlockSpec((1, c, hd), lambda bb, j: (bb, j, 0))
    whole = lambda shape: pl.BlockSpec(shape, lambda bb, j: (0, 0))
    est = 2 * (2 * c * hd * 4 + 4 * r * hd * 2) + 6 * r * sub * 4 + 3 * r * 2 * hd * 4
    out = pl.pallas_call(
        functools.partial(_fox_cached_kernel, sub=sub, hd=hd),
        out_shape=jax.ShapeDtypeStruct((b, r, hd), BF16),
        grid=(b, p // tp),
        in_specs=[new, past, past, new, new,
                  pl.BlockSpec((1, 1, c), lambda bb, j: (bb, 0, j)),
                  pl.BlockSpec((1, 1, r), lambda bb, j: (bb, 0, 0)),
                  whole((r, 1)), whole((r, 1)), whole((1, c)), whole((1, r))],
        out_specs=new,
        scratch_shapes=[pltpu.VMEM((r, LANE), F32), pltpu.VMEM((r, 2 * hd), F32)],
        compiler_params=_params(("parallel", "arbitrary"), est),
        name="fox_cached",
    )(q.reshape(b, r, hd), k_past.reshape(b, p * nh, hd), v_past.reshape(b, p * nh, hd),
      k_new.reshape(b, r, hd), v_new.reshape(b, r, hd),
      f_all[:, :p].reshape(b, 1, p * nh), f_all[:, p:].reshape(b, 1, r),
      heads[:r].reshape(r, 1), times.reshape(r, 1), heads.reshape(1, c), times.reshape(1, r))
    return out.reshape(b, n, dm)


def _trunk(x, pool_prev, mem_k, mem_v, past, w):
    b, n, d = x.shape
    m = b * n
    depth, n_a = w["depth"], w["n_a"]
    c, hd, heads = w["d_pool"], w["hd_fox"], w["mem_heads"]
    assert n >= POOL_STATE
    pos0 = 0 if past is None else past[0].shape[1]
    x2 = x.reshape(m, d)
    h = rms_cast(x2, w["norm_g"][0, 0])
    new_pool = []
    k_new = v_new = logf_new = None
    k_bf = v_bf = f_all = None
    for l in range(depth):
        g = w["norm_g"][l]
        a = ffn_up(h, w["wg"][l][0], w["wu"][l][0], w["d_ff"])
        x2, (h,) = mm_resid([a], w["wd"][l][0], x2, g[1], 0.5, [g[2]])
        if l < n_a:
            w_in, w_out = w["w_in_a"][l], w["w_out_a"][l]
            (u,) = mm(h, w_in, [F32], cols=(0, c))
            u3 = u.reshape(b, n, c)
            main = pool_mixer(u3, pool_prev[l], pos0, w["pool_w"][l], w["pool_scale"][l])
            new_pool.append(u3[:, n - POOL_STATE:, :])
        else:
            j = l - n_a
            w_in, w_out = w["w_in_b"][j], w["w_out_b"][j]
            (q,) = mm(h, w_in, [BF16], scale=LOG2E * hd ** -0.5, cols=(0, c))
            if past is None:
                main = fox_prompt(q, k_bf, v_bf, f_all[0], hd).reshape(b, n, c)
            else:
                main = fox_cached(q.reshape(b, n, c), past[0], past[1], k_bf.reshape(b, n, c),
                                  v_bf.reshape(b, n, c), f_all, hd)
        (qm,) = mm(h, w_in, [BF16], cols=(c, d))
        mem = mem_attend(qm.reshape(b, n, d - c), mem_k[l], mem_v[l], heads)
        x2, (h,) = mm_resid([main.reshape(m, c), mem.reshape(m, d - c)], w_out, x2, g[3], 1.0, [g[4]])
        a = ffn_up(h, w["wg"][l][1], w["wu"][l][1], w["d_ff"])
        g_next = []
        if l + 1 < depth:
            g_next.append(w["norm_g"][l + 1, 0])
        if l == n_a - 1:
            g_next.append(w["g_kv"])
        x2, hs = mm_resid([a], w["wd"][l][1], x2, g[5], 0.5, g_next)
        if l + 1 < depth:
            h = hs[0]
        if l == n_a - 1:
            hk = hs[-1]
            k_new, k_bf = mm(hk, w["w_kv"], [F32, BF16], cols=(0, c))
            v_new, v_bf = mm(hk, w["w_kv"], [F32, BF16], cols=(c, 2 * c))
            logf_new = log_forget(hk, w["w_f"], w["b_f"]).reshape(b, n, -1)
            logf_all = logf_new if past is None else jnp.concatenate(
                [past[2].astype(F32), logf_new], axis=1)
            f_all = cumsum_time(logf_all)
    nh = c // hd
    return (x2.reshape(b, n, d), jnp.stack(new_pool), k_new.reshape(b, n, nh, hd),
            v_new.reshape(b, n, nh, hd), logf_new)


def _cast_pad_kernel(x_ref, o_ref, *, rows, cols):
    tr, tc = o_ref.shape
    r = pl.program_id(0) * tr + lax.broadcasted_iota(jnp.int32, (tr, tc), 0)
    c = pl.program_id(1) * tc + lax.broadcasted_iota(jnp.int32, (tr, tc), 1)
    o_ref[...] = jnp.where((r < rows) & (c < cols), x_ref[...], 0.0).astype(o_ref.dtype)


def _pad_ff(w4, l, s, axis):
    rows, cols = w4.shape[2:]
    shape = [rows, cols]
    shape[axis] += -shape[axis] % FF_ALIGN
    tr = _tile(shape[0], CAST_TILE_ELEMS // MM_TILE)
    tc = _tile(shape[1], CAST_TILE_ELEMS // tr)
    return pl.pallas_call(
        functools.partial(_cast_pad_kernel, rows=rows, cols=cols),
        out_shape=jax.ShapeDtypeStruct(tuple(shape), BF16),
        grid=(shape[0] // tr, shape[1] // tc),
        in_specs=[pl.BlockSpec((None, None, tr, tc), lambda i, j: (l, s, i, j))],
        out_specs=pl.BlockSpec((tr, tc), lambda i, j: (i, j)),
        compiler_params=_params(("parallel", "parallel"), 2 * tr * tc * 6 + 3 * tr * tc * 4),
        name="cast_pad",
    )(w4)


def kernel(x_prompt, x_sample, mem_prompt, cache_fox_k, cache_fox_v, cache_fox_logf, cache_mem_k, cache_mem_v, state_pool, norm_g, w_ffn_gate, w_ffn_up, w_ffn_down, w_in_a, pool_w, pool_scale, w_out_a, w_in_b, w_out_b, g_kv, w_kv, w_f, b_f, g_mem, w_mem_kv):
    depth = norm_g.shape[0]
    n_a, db, _, c = state_pool.shape
    hd = cache_fox_k.shape[3]
    _, _, n_mem, heads, mem_hd = cache_mem_k.shape
    dq = heads * mem_hd
    bp, _, d = x_prompt.shape
    w = dict(
        depth=depth, n_a=n_a, d_pool=c, hd_fox=hd, mem_heads=heads, d_ff=w_ffn_gate.shape[3],
        norm_g=norm_g, g_kv=g_kv, w_f=w_f, b_f=b_f, pool_scale=pool_scale,
        wg=[[_pad_ff(w_ffn_gate, l, s, 1) for s in range(2)] for l in range(depth)],
        wu=[[_pad_ff(w_ffn_up, l, s, 1) for s in range(2)] for l in range(depth)],
        wd=[[_pad_ff(w_ffn_down, l, s, 0) for s in range(2)] for l in range(depth)],
        w_in_a=w_in_a.astype(BF16), w_out_a=w_out_a.astype(BF16),
        w_in_b=w_in_b.astype(BF16), w_out_b=w_out_b.astype(BF16),
        pool_w=pool_w.astype(BF16), w_kv=w_kv.astype(BF16),
    )
    mem2 = mem_prompt.reshape(bp * n_mem, d)
    mem_kv = [mm(rms_cast(mem2, g_mem[l]), w_mem_kv[l].astype(BF16), [F32])[0] for l in range(depth)]
    mem_k_p = jnp.stack([kv[:, :dq].reshape(bp, n_mem, dq) for kv in mem_kv])
    mem_v_p = jnp.stack([kv[:, dq:].reshape(bp, n_mem, dq) for kv in mem_kv])

    zero_prev = jnp.zeros((n_a, bp, HALO, c), F32)
    y_p, pool_p, k_p, v_p, logf_p = _trunk(x_prompt, zero_prev, mem_k_p, mem_v_p, None, w)

    prev_s = jnp.pad(state_pool.astype(F32), ((0, 0), (0, 0), (1, 0), (0, 0)))
    past = (cache_fox_k, cache_fox_v, cache_fox_logf)
    y_s, pool_s, k_s, v_s, logf_s = _trunk(
        x_sample, prev_s, cache_mem_k.reshape(depth, db, n_mem, dq),
        cache_mem_v.reshape(depth, db, n_mem, dq), past, w)

    return (y_p, y_s, k_p, v_p, logf_p,
            mem_k_p.reshape(depth, bp, n_mem, heads, mem_hd),
            mem_v_p.reshape(depth, bp, n_mem, heads, mem_hd),
            pool_p, k_s, v_s, logf_s, pool_s)
```

```python
import functools

import jax
import jax.numpy as jnp
from jax import lax
from jax.experimental import pallas as pl
from jax.experimental.pallas import tpu as pltpu

EPS = 1e-6
POOL_WINDOWS = (2, 4, 8, 16)
POOL_STATE = max(POOL_WINDOWS) - 1
HALO = POOL_STATE + 1
LANE = 128
BF16_ROWS = 16
MASK_VALUE = -1e30
VMEM_CAP = 60 * 1024 * 1024
VMEM_FLOOR = 32 * 1024 * 1024
VMEM_SLACK = 4 * 1024 * 1024
F32 = jnp.float32
BF16 = jnp.bfloat16

MM_TILE = 1024
FFN_UP_TILE_N = 512
FF_ALIGN = 1024
RESID_TILE_M, RESID_TILE_K = 512, 1024
RESID_MAX_CHUNKS = 8
ROW_TILE = 512
CAST_TILE_ELEMS = 2 * 1024 * 1024
POOL_TILE = 512
FOX_TILE_Q = 1024
FOX_PAST_TILE = 256
CUMSUM_TILE = 256


def _tile(n, pref):
    if n <= pref:
        return n
    t = pref
    while n % t:
        t //= 2
    return t


def _params(sem, est_bytes):
    limit = int(min(VMEM_CAP, max(VMEM_FLOOR, est_bytes * 5 // 4 + VMEM_SLACK)))
    return pltpu.CompilerParams(dimension_semantics=sem, vmem_limit_bytes=limit)


def _rms_normalize(xf):
    return xf * lax.rsqrt(jnp.mean(xf * xf, axis=-1, keepdims=True) + EPS)


def _rms_cast_kernel(x_ref, g_ref, o_ref):
    o_ref[...] = (_rms_normalize(x_ref[...]) * g_ref[...]).astype(o_ref.dtype)


def rms_cast(x, g):
    m, d = x.shape
    tm = _tile(m, ROW_TILE)
    return pl.pallas_call(
        _rms_cast_kernel,
        out_shape=jax.ShapeDtypeStruct((m, d), BF16),
        grid=(m // tm,),
        in_specs=[pl.BlockSpec((tm, d), lambda i: (i, 0)),
                  pl.BlockSpec((1, d), lambda i: (0, 0))],
        out_specs=pl.BlockSpec((tm, d), lambda i: (i, 0)),
        compiler_params=_params(("parallel",), 2 * tm * d * 6),
        name="rms_cast",
    )(x, g.reshape(1, d).astype(F32))


def _mm_kernel(h_ref, w_ref, *o_refs, scale):
    acc = jnp.dot(h_ref[...], w_ref[...], preferred_element_type=F32)
    if scale is not None:
        acc = acc * scale
    for o_ref in o_refs:
        o_ref[...] = acc.astype(o_ref.dtype)


def mm(h, w, out_dtypes, scale=None, cols=None):
    m, k = h.shape
    c0, c1 = cols if cols is not None else (0, w.shape[1])
    n = c1 - c0
    tm, tn = _tile(m, MM_TILE), _tile(n, MM_TILE)
    assert c0 % tn == 0
    j0 = c0 // tn
    est = 2 * (tm * k * 2 + k * tn * 2) + tm * tn * 4 + sum(
        2 * tm * tn * jnp.dtype(t).itemsize for t in out_dtypes)
    outs = pl.pallas_call(
        functools.partial(_mm_kernel, scale=scale),
        out_shape=[jax.ShapeDtypeStruct((m, n), t) for t in out_dtypes],
        grid=(m // tm, n // tn),
        in_specs=[pl.BlockSpec((tm, k), lambda i, j: (i, 0)),
                  pl.BlockSpec((k, tn), lambda i, j: (0, j + j0))],
        out_specs=[pl.BlockSpec((tm, tn), lambda i, j: (i, j)) for _ in out_dtypes],
        compiler_params=_params(("parallel", "parallel"), est),
        name="mm",
    )(h, w)
    return outs


def _ffn_up_kernel(h_ref, wg_ref, wu_ref, o_ref, *, full_blocks, part):
    j = pl.program_id(1)
    tn = o_ref.shape[1]

    def swiglu(width):
        h = h_ref[...]
        g = jnp.dot(h, wg_ref[:, :width], preferred_element_type=F32)
        u = jnp.dot(h, wu_ref[:, :width], preferred_element_type=F32)
        o_ref[:, :width] = (g * jax.nn.sigmoid(g) * u).astype(o_ref.dtype)
        if width < tn:
            o_ref[:, width:] = jnp.zeros((o_ref.shape[0], tn - width), o_ref.dtype)

    @pl.when(j < full_blocks)
    def _():
        swiglu(tn)

    if part:
        @pl.when(j == full_blocks)
        def _():
            swiglu(part)

    @pl.when(j >= full_blocks + (1 if part else 0))
    def _():
        o_ref[...] = jnp.zeros_like(o_ref)


def ffn_up(h, wg, wu, n_real):
    m, k = h.shape
    n = wg.shape[1]
    tm, tn = _tile(m, MM_TILE), _tile(n, FFN_UP_TILE_N)
    part = -(-(n_real % tn) // LANE) * LANE
    est = 2 * (tm * k * 2 + 2 * k * tn * 2 + tm * tn * 2) + 3 * tm * tn * 4
    return pl.pallas_call(
        functools.partial(_ffn_up_kernel, full_blocks=n_real // tn, part=part),
        out_shape=jax.ShapeDtypeStruct((m, n), BF16),
        grid=(m // tm, n // tn),
        in_specs=[pl.BlockSpec((tm, k), lambda i, j: (i, 0)),
                  pl.BlockSpec((k, tn), lambda i, j: (0, j)),
                  pl.BlockSpec((k, tn), lambda i, j: (0, j))],
        out_specs=pl.BlockSpec((tm, tn), lambda i, j: (i, j)),
        compiler_params=_params(("parallel", "parallel"), est),
        name="ffn_up",
    )(h, wg, wu)


def _mm_resid_kernel(*refs, kblocks, coef, n_next, ntiles, nchunk):
    n_lhs = len(kblocks)
    a_refs = refs[:n_lhs]
    w_ref, x_ref, gp_ref = refs[n_lhs:n_lhs + 3]
    gn_refs = refs[n_lhs + 3:n_lhs + 3 + n_next]
    xo_ref = refs[n_lhs + 3 + n_next]
    h_refs = refs[n_lhs + 4 + n_next:-2]
    acc_refs = refs[-2:]
    i, k = pl.program_id(0), pl.program_id(1)
    rows = xo_ref.shape[0]

    @pl.when((i == 0) & (k == 0))
    def _():
        for acc_ref in acc_refs:
            acc_ref[...] = jnp.zeros_like(acc_ref)

    def accumulate(acc_ref):
        a = a_refs[0][...]
        start = kblocks[0]
        for a_ref, nb in zip(a_refs[1:], kblocks[1:]):
            a = jnp.where(k >= start, a_ref[...], a)
            start += nb
        prev = jnp.where(k == 0, 0.0, acc_ref[...])
        acc_ref[...] = prev + jnp.dot(a, w_ref[...], preferred_element_type=F32)

    def epilogue(acc_ref):
        r0 = pl.multiple_of(jnp.minimum(k, nchunk - 1) * rows, rows)
        y = acc_ref[pl.ds(r0, rows), :]
        xn = x_ref[...] + coef * (_rms_normalize(y) * gp_ref[...])
        xo_ref[...] = xn
        if n_next:
            r = _rms_normalize(xn)
            for gn_ref, h_ref in zip(gn_refs, h_refs):
                h_ref[...] = (r * gn_ref[...]).astype(h_ref.dtype)

    for parity in range(2):
        cur, prev = acc_refs[parity], acc_refs[1 - parity]

        @pl.when((i % 2 == parity) & (i < ntiles))
        def _(cur=cur, prev=prev):
            accumulate(cur)
            epilogue(prev)

    @pl.when(i == ntiles)
    def _():
        epilogue(acc_refs[1 - ntiles % 2])


def mm_resid(lhs, w, x, g_post, coef, g_next):
    m, d = x.shape
    widths = [a.shape[1] for a in lhs]
    assert sum(widths) == w.shape[0] and w.shape[1] == d
    tk = RESID_TILE_K
    while any(kw % tk for kw in widths):
        tk //= 2
    tm = _tile(m, RESID_TILE_M)
    ntiles = m // tm
    kblocks = tuple(kw // tk for kw in widths)
    starts = [sum(kblocks[:j]) for j in range(len(lhs))]
    n_next = len(g_next)
    nk = sum(kblocks)
    nchunk = 1
    while nchunk * 2 <= min(nk, RESID_MAX_CHUNKS) and tm % (nchunk * 2 * BF16_ROWS) == 0:
        nchunk *= 2
    rows = tm // nchunk
    gain = pl.BlockSpec((1, d), lambda i, k: (0, 0))

    def step(i, k):
        return jnp.where(i == ntiles, nk - 1, k)

    def lhs_spec(start, nb):
        return pl.BlockSpec(
            (tm, tk),
            lambda i, k: (jnp.minimum(i, ntiles - 1), jnp.clip(step(i, k) - start, 0, nb - 1)))

    chunk = pl.BlockSpec(
        (rows, d), lambda i, k: (jnp.maximum((i - 1) * nchunk + jnp.minimum(k, nchunk - 1), 0), 0))
    est = len(lhs) * 2 * tm * tk * 2 + 2 * tk * d * 2 + 2 * tm * d * 4 \
        + 2 * rows * d * (8 + 2 * n_next) + 8 * rows * d * 4 + tm * tk * 2
    outs = pl.pallas_call(
        functools.partial(_mm_resid_kernel, kblocks=kblocks, coef=coef, n_next=n_next,
                          ntiles=ntiles, nchunk=nchunk),
        out_shape=[jax.ShapeDtypeStruct((m, d), F32)]
        + [jax.ShapeDtypeStruct((m, d), BF16) for _ in range(n_next)],
        grid=(ntiles + 1, nk),
        in_specs=[lhs_spec(s, nb) for s, nb in zip(starts, kblocks)]
        + [pl.BlockSpec((tk, d), lambda i, k: (step(i, k), 0)), chunk, gain] + [gain] * n_next,
        out_specs=[chunk] * (1 + n_next),
        scratch_shapes=[pltpu.VMEM((tm, d), F32), pltpu.VMEM((tm, d), F32)],
        compiler_params=_params(("arbitrary", "arbitrary"), est),
        name="mm_resid",
    )(*lhs, w, x, g_post.reshape(1, d).astype(F32), *[g.reshape(1, d).astype(F32) for g in g_next])
    return outs[0], outs[1:]


def _pool_kernel(u_ref, halo_ref, prev_ref, w_ref, sc_ref, o_ref, ext_ref, *, tm, pos0, group):
    i = pl.program_id(1)

    @pl.when(i == 0)
    def _():
        ext_ref[0:HALO, :] = prev_ref[0]

    @pl.when(i > 0)
    def _():
        ext_ref[0:HALO, :] = halo_ref[0]

    ext_ref[HALO:HALO + tm, :] = u_ref[0]
    pos = pos0 + i * tm + lax.broadcasted_iota(jnp.int32, (tm, 1), 0)
    for gi, win in enumerate(POOL_WINDOWS):
        sl = slice(gi * group, (gi + 1) * group)
        ug = ext_ref[HALO:HALO + tm, sl]
        wsum = ug
        for j in range(1, win):
            wsum = wsum + ext_ref[HALO - j:HALO - j + tm, sl]
        cnt = jnp.minimum(pos + 1, win).astype(F32)
        diff = wsum / cnt - ug
        y = jnp.dot(diff.astype(BF16), w_ref[gi], preferred_element_type=F32)
        o_ref[0, :, sl] = (y * sc_ref[:, sl]).astype(o_ref.dtype)


def pool_mixer(u, prev, pos0, pool_w, pool_scale):
    b, n, c = u.shape
    group = c // len(POOL_WINDOWS)
    tm = _tile(n, POOL_TILE)
    steps = tm // HALO
    est = 2 * (tm * c * 4 + 2 * HALO * c * 4 + pool_w.size * 2 + tm * c * 2) + (tm + HALO) * c * 4 \
        + 4 * tm * group * 4
    return pl.pallas_call(
        functools.partial(_pool_kernel, tm=tm, pos0=pos0, group=group),
        out_shape=jax.ShapeDtypeStruct((b, n, c), BF16),
        grid=(b, n // tm),
        in_specs=[pl.BlockSpec((1, tm, c), lambda bb, i: (bb, i, 0)),
                  pl.BlockSpec((1, HALO, c), lambda bb, i: (bb, jnp.maximum(i * steps - 1, 0), 0)),
                  pl.BlockSpec((1, HALO, c), lambda bb, i: (bb, 0, 0)),
                  pl.BlockSpec(pool_w.shape, lambda bb, i: (0, 0, 0)),
                  pl.BlockSpec((1, c), lambda bb, i: (0, 0))],
        out_specs=pl.BlockSpec((1, tm, c), lambda bb, i: (bb, i, 0)),
        scratch_shapes=[pltpu.VMEM((tm + HALO, c), F32)],
        compiler_params=_params(("parallel", "parallel"), est),
        name="pool_mixer",
    )(u, u, prev, pool_w, pool_scale.reshape(1, c).astype(F32))


def _mem_attn_kernel(q_ref, k_ref, v_ref, o_ref, *, heads, hd):
    scale = hd ** -0.5
    for hh in range(heads):
        sl = slice(hh * hd, (hh + 1) * hd)
        q = q_ref[0, :, sl]
        k = k_ref[0, :, sl].astype(BF16)
        v = v_ref[0, :, sl].astype(BF16)
        s = lax.dot_general(q, k, (((1,), (1,)), ((), ())), preferred_element_type=F32) * scale
        p = jnp.exp(s - jnp.max(s, axis=-1, keepdims=True))
        p = p / jnp.sum(p, axis=-1, keepdims=True)
        o_ref[0, :, sl] = jnp.dot(p.astype(BF16), v, preferred_element_type=F32).astype(o_ref.dtype)


def _proj_mem_attn_kernel(h_ref, w_ref, k_ref, v_ref, o_ref, *, heads, hd):
    qf = jnp.dot(h_ref[...], w_ref[...], preferred_element_type=F32)
    scale = hd ** -0.5
    for hh in range(heads):
        sl = slice(hh * hd, (hh + 1) * hd)
        q = qf[:, sl].astype(BF16)
        k = k_ref[:, sl].astype(BF16)
        v = v_ref[:, sl].astype(BF16)
        s = lax.dot_general(q, k, (((1,), (1,)), ((), ())), preferred_element_type=F32) * scale
        p = jnp.exp(s - jnp.max(s, axis=-1, keepdims=True))
        p = p / jnp.sum(p, axis=-1, keepdims=True)
        o_ref[:, sl] = jnp.dot(p.astype(BF16), v, preferred_element_type=F32).astype(o_ref.dtype)


def proj_mem_attend(h, w, c0, mk, mv, heads):
    n, k = h.shape
    nm, dq = mk.shape
    tm = _tile(n, ROW_TILE)
    assert c0 % dq == 0 and w.shape[1] == c0 + dq
    j0 = c0 // dq
    est = 2 * (tm * k * 2 + k * dq * 2 + 2 * nm * dq * 4 + tm * dq * 2) + tm * dq * 4 + 4 * tm * nm * 4
    return pl.pallas_call(
        functools.partial(_proj_mem_attn_kernel, heads=heads, hd=dq // heads),
        out_shape=jax.ShapeDtypeStruct((n, dq), BF16),
        grid=(n // tm,),
        in_specs=[pl.BlockSpec((tm, k), lambda i: (i, 0)),
                  pl.BlockSpec((k, dq), lambda i: (0, j0)),
                  pl.BlockSpec((nm, dq), lambda i: (0, 0)),
                  pl.BlockSpec((nm, dq), lambda i: (0, 0))],
        out_specs=pl.BlockSpec((tm, dq), lambda i: (i, 0)),
        compiler_params=_params(("parallel",), est),
        name="proj_mem_attend",
    )(h, w, mk, mv)


def mem_attend(qm, mk, mv, heads):
    b, n, dq = qm.shape
    nm = mk.shape[1]
    tm = _tile(n, ROW_TILE)
    est = 2 * (2 * tm * dq * 2 + 2 * nm * dq * 4) + 4 * tm * nm * 4
    return pl.pallas_call(
        functools.partial(_mem_attn_kernel, heads=heads, hd=dq // heads),
        out_shape=jax.ShapeDtypeStruct((b, n, dq), BF16),
        grid=(b, n // tm),
        in_specs=[pl.BlockSpec((1, tm, dq), lambda bb, i: (bb, i, 0)),
                  pl.BlockSpec((1, nm, dq), lambda bb, i: (bb, 0, 0)),
                  pl.BlockSpec((1, nm, dq), lambda bb, i: (bb, 0, 0))],
        out_specs=pl.BlockSpec((1, tm, dq), lambda bb, i: (bb, i, 0)),
        compiler_params=_params(("parallel", "parallel"), est),
        name="mem_attend",
    )(qm, mk, mv)


def _logf_kernel(h_ref, w_ref, b_ref, o_ref):
    z = jnp.dot(h_ref[...], w_ref[...], preferred_element_type=F32) + b_ref[...]
    o_ref[...] = jnp.minimum(z, 0.0) - jnp.log1p(jnp.exp(-jnp.abs(z)))


def log_forget(hk, w_f, b_f):
    m, k = hk.shape
    nh = w_f.shape[1]
    wp = jnp.pad(w_f, ((0, 0), (0, LANE - nh))).astype(BF16)
    bp = jnp.pad(b_f, (0, LANE - nh)).reshape(1, LANE).astype(F32)
    tm = _tile(m, MM_TILE)
    out = pl.pallas_call(
        _logf_kernel,
        out_shape=jax.ShapeDtypeStruct((m, LANE), F32),
        grid=(m // tm,),
        in_specs=[pl.BlockSpec((tm, k), lambda i: (i, 0)),
                  pl.BlockSpec((k, LANE), lambda i: (0, 0)),
                  pl.BlockSpec((1, LANE), lambda i: (0, 0))],
        out_specs=pl.BlockSpec((tm, LANE), lambda i: (i, 0)),
        compiler_params=_params(("parallel",), 2 * (tm * k * 2 + k * LANE * 2 + tm * LANE * 4)),
        name="log_forget",
    )(hk, wp, bp)
    return out[:, :nh]


def _cumsum_kernel(x_ref, o_ref, carry_ref, *, tb):
    @pl.when(pl.program_id(1) == 0)
    def _():
        carry_ref[...] = jnp.zeros_like(carry_ref)

    x = x_ref[0]
    row = lax.broadcasted_iota(jnp.int32, x.shape, 0)
    d = 1
    while d < tb:
        x = x + jnp.where(row >= d, pltpu.roll(x, d, 0), 0.0)
        d *= 2
    x = x + carry_ref[0:1, :]
    o_ref[0] = x
    carry_ref[...] = jnp.broadcast_to(x[tb - 1:tb, :], carry_ref.shape)


def cumsum_time(logf):
    b, t, nh = logf.shape
    tb = CUMSUM_TILE
    tp = -(-t // tb) * tb
    x = jnp.pad(logf, ((0, 0), (0, tp - t), (0, LANE - nh)))
    out = pl.pallas_call(
        functools.partial(_cumsum_kernel, tb=tb),
        out_shape=jax.ShapeDtypeStruct((b, tp, LANE), F32),
        grid=(b, tp // tb),
        in_specs=[pl.BlockSpec((1, tb, LANE), lambda bb, i: (bb, i, 0))],
        out_specs=pl.BlockSpec((1, tb, LANE), lambda bb, i: (bb, i, 0)),
        scratch_shapes=[pltpu.VMEM((8, LANE), F32)],
        compiler_params=_params(("parallel", "arbitrary"), 8 * tb * LANE * 4),
        name="cumsum_time",
    )(x)
    return out[:, :t, :nh]


LOG2E = 1.4426950408889634
_NT = (((1,), (1,)), ((), ()))


def _lanes(x, reps):
    return x if reps == 1 else jnp.concatenate([x] * reps, axis=1)


def _fox_prompt_kernel(q_ref, k_ref, v_ref, f_ref, o_ref, s0_ref, s1_ref, m_ref, acc_ref,
                       *, tq, tk, hd, nhb):
    qi = pl.program_id(1)
    m_ref[...] = jnp.full_like(m_ref, MASK_VALUE)
    acc_ref[...] = jnp.zeros_like(acc_ref)
    ones = jnp.ones((tk, hd), BF16)

    def logits(kb, s_ref, r0=0):
        off = pl.multiple_of(kb * tk, tk)
        for hh in range(nhb):
            sl = slice(hh * hd, (hh + 1) * hd)
            s = lax.dot_general(q_ref[r0:, sl], k_ref[pl.ds(off, tk), sl], _NT,
                                preferred_element_type=F32)
            s_ref[hh, r0:, :] = s - f_ref[hh, kb] * LOG2E

    def attend(kb, s_ref, diag=None):
        off = pl.multiple_of(kb * tk, tk)
        r0 = 0 if diag is None else diag * tk
        for hh in range(nhb):
            sl = slice(hh * hd, (hh + 1) * hd)
            s = s_ref[hh, r0:, :]
            if diag is not None:
                row = lax.broadcasted_iota(jnp.int32, s.shape, 0)
                col = lax.broadcasted_iota(jnp.int32, s.shape, 1)
                s = jnp.where(col <= row, s, MASK_VALUE)
            m_prev = m_ref[hh, r0:, :]
            m_new = jnp.maximum(m_prev, jnp.max(s, axis=-1, keepdims=True))
            alpha = jnp.exp2(m_prev - m_new)
            p = jnp.exp2(s - _lanes(m_new, tk // LANE))
            v = v_ref[pl.ds(off, tk), sl]
            pv = jnp.dot(p.astype(BF16), jnp.concatenate([v, ones], axis=1),
                         preferred_element_type=F32)
            acc_ref[hh, r0:, :] = _lanes(alpha, 2 * hd // LANE) * acc_ref[hh, r0:, :] + pv
            m_ref[hh, r0:, :] = m_new

    def body(jj, carry):
        kb = 2 * jj
        logits(kb + 1, s1_ref)
        attend(kb, s0_ref)
        logits(kb + 2, s0_ref)
        attend(kb + 1, s1_ref)
        return carry

    logits(0, s0_ref)
    lax.fori_loop(0, qi, body, 0)
    logits(2 * qi + 1, s1_ref, r0=tk)
    attend(2 * qi, s0_ref, diag=0)
    attend(2 * qi + 1, s1_ref, diag=1)

    for hh in range(nhb):
        acc = acc_ref[hh]
        o_ref[:, hh * hd:(hh + 1) * hd] = (acc[:, :hd] / acc[:, hd:]).astype(o_ref.dtype)


def fox_prompt(q, k, v, f_cum, hd):
    n, dm = q.shape
    nh = dm // hd
    nhb = 2 if nh % 2 == 0 else 1
    tq = _tile(n, FOX_TILE_Q)
    tk = tq // 2
    assert hd == LANE and tk % LANE == 0
    f_t = f_cum.T.reshape(nh, n // tk, 1, tk)
    wb = nhb * hd
    est = 2 * (2 * tq * wb * 2 + 2 * n * wb * 2 + nhb * (n // tk) * 8 * tk * 4) \
        + nhb * (tq * LANE * 4 + tq * 2 * hd * 4 + 5 * tq * tk * 4)
    return pl.pallas_call(
        functools.partial(_fox_prompt_kernel, tq=tq, tk=tk, hd=hd, nhb=nhb),
        out_shape=jax.ShapeDtypeStruct((n, dm), BF16),
        grid=(nh // nhb, n // tq),
        in_specs=[pl.BlockSpec((tq, wb), lambda h, i: (i, h)),
                  pl.BlockSpec((n, wb), lambda h, i: (0, h)),
                  pl.BlockSpec((n, wb), lambda h, i: (0, h)),
                  pl.BlockSpec((nhb, n // tk, 1, tk), lambda h, i: (h, 0, 0, 0))],
        out_specs=pl.BlockSpec((tq, wb), lambda h, i: (i, h)),
        scratch_shapes=[pltpu.VMEM((nhb, tq, tk), F32), pltpu.VMEM((nhb, tq, tk), F32),
                        pltpu.VMEM((nhb, tq, LANE), F32), pltpu.VMEM((nhb, tq, 2 * hd), F32)],
        compiler_params=_params(("parallel", "parallel"), est),
        name="fox_prompt",
    )(q, k, v, f_t)


def _fox_cached_kernel(q_ref, kp_ref, vp_ref, kn_ref, vn_ref, fp_ref, fn_ref, rh_ref, rt_ref,
                       ch_ref, ct_ref, o_ref, m_ref, acc_ref, *, sub, hd):
    j = pl.program_id(1)
    q = q_ref[0]
    rh = rh_ref[...]

    @pl.when(j == 0)
    def _():
        m_ref[...] = jnp.full_like(m_ref, MASK_VALUE)
        acc_ref[...] = jnp.zeros_like(acc_ref)

    def logits(k, f, valid):
        s = lax.dot_general(q, k, _NT, preferred_element_type=F32) - f * LOG2E
        return jnp.where(valid, s, MASK_VALUE)

    def absorb(s, v):
        width = s.shape[1]
        m_prev = m_ref[...]
        m_new = jnp.maximum(m_prev, jnp.max(s, axis=-1, keepdims=True))
        alpha = jnp.exp2(m_prev - m_new)
        p = jnp.exp2(s - (_lanes(m_new, width // LANE) if width % LANE == 0 else m_new[:, :1]))
        vaug = jnp.concatenate([v, jnp.ones_like(v)], axis=1)
        pv = jnp.dot(p.astype(BF16), vaug, preferred_element_type=F32)
        acc_ref[...] = _lanes(alpha, 2 * hd // LANE) * acc_ref[...] + pv
        m_ref[...] = m_new

    subs = [slice(c0, c0 + sub) for c0 in range(0, kp_ref.shape[1], sub)]
    s_all = [logits(kp_ref[0, cs, :].astype(BF16), fp_ref[0, :, cs], rh == ch_ref[:, cs])
             for cs in subs]
    for s, cs in zip(s_all, subs):
        absorb(s, vp_ref[0, cs, :].astype(BF16))

    @pl.when(j == pl.num_programs(1) - 1)
    def _():
        r = kn_ref.shape[1]
        absorb(logits(kn_ref[0], fn_ref[0], (rh == ch_ref[:, :r]) & (ct_ref[...] <= rt_ref[...])),
               vn_ref[0])
        acc = acc_ref[...]
        o_ref[0] = (acc[:, :hd] / acc[:, hd:]).astype(o_ref.dtype)


def fox_cached(q, k_past, v_past, k_new, v_new, f_all, hd):
    b, n, dm = q.shape
    p, nh = k_past.shape[1:3]
    r = n * nh
    tp = _tile(p, FOX_PAST_TILE)
    c = tp * nh
    sub = c // 2 if c % (2 * LANE * nh) == 0 else c
    assert c % LANE == 0 and c >= r
    heads = jnp.arange(c, dtype=jnp.int32) % nh
    times = jnp.arange(r, dtype=jnp.int32) // nh
    new = pl.BlockSpec((1, r, hd), lambda bb, j: (bb, 0, 0))
    past = pl.BlockSpec((1, c, hd), lambda bb, j: (bb, j, 0))
    whole = lambda shape: pl.BlockSpec(shape, lambda bb, j: (0, 0))
    est = 2 * (2 * c * hd * 4 + 4 * r * hd * 2) + 6 * r * sub * 4 + 3 * r * 2 * hd * 4
    out = pl.pallas_call(
        functools.partial(_fox_cached_kernel, sub=sub, hd=hd),
        out_shape=jax.ShapeDtypeStruct((b, r, hd), BF16),
        grid=(b, p // tp),
        in_specs=[new, past, past, new, new,
                  pl.BlockSpec((1, 1, c), lambda bb, j: (bb, 0, j)),
                  pl.BlockSpec((1, 1, r), lambda bb, j: (bb, 0, 0)),
                  whole((r, 1)), whole((r, 1)), whole((1, c)), whole((1, r))],
        out_specs=new,
        scratch_shapes=[pltpu.VMEM((r, LANE), F32), pltpu.VMEM((r, 2 * hd), F32)],
        compiler_params=_params(("parallel", "arbitrary"), est),
        name="fox_cached",
    )(q.reshape(b, r, hd), k_past.reshape(b, p * nh, hd), v_past.reshape(b, p * nh, hd),
      k_new.reshape(b, r, hd), v_new.reshape(b, r, hd),
      f_all[:, :p].reshape(b, 1, p * nh), f_all[:, p:].reshape(b, 1, r),
      heads[:r].reshape(r, 1), times.reshape(r, 1), heads.reshape(1, c), times.reshape(1, r))
    return out.reshape(b, n, dm)


def _trunk(x, pool_prev, mem_k, mem_v, past, w):
    b, n, d = x.shape
    m = b * n
    depth, n_a = w["depth"], w["n_a"]
    c, hd, heads = w["d_pool"], w["hd_fox"], w["mem_heads"]
    assert n >= POOL_STATE
    pos0 = 0 if past is None else past[0].shape[1]
    x2 = x.reshape(m, d)
    h = rms_cast(x2, w["norm_g"][0, 0])
    new_pool = []
    k_new = v_new = logf_new = None
    k_bf = v_bf = f_all = None
    for l in range(depth):
        g = w["norm_g"][l]
        a = ffn_up(h, w["wg"][l][0], w["wu"][l][0], w["d_ff"])
        x2, (h,) = mm_resid([a], w["wd"][l][0], x2, g[1], 0.5, [g[2]])
        if l < n_a:
            w_in, w_out = w["w_in_a"][l], w["w_out_a"][l]
            (u,) = mm(h, w_in, [F32], cols=(0, c))
            u3 = u.reshape(b, n, c)
            main = pool_mixer(u3, pool_prev[l], pos0, w["pool_w"][l], w["pool_scale"][l])
            new_pool.append(u3[:, n - POOL_STATE:, :])
        else:
            j = l - n_a
            w_in, w_out = w["w_in_b"][j], w["w_out_b"][j]
            (q,) = mm(h, w_in, [BF16], scale=LOG2E * hd ** -0.5, cols=(0, c))
            if past is None:
                main = fox_prompt(q, k_bf, v_bf, f_all[0], hd).reshape(b, n, c)
            else:
                main = fox_cached(q.reshape(b, n, c), past[0], past[1], k_bf.reshape(b, n, c),
                                  v_bf.reshape(b, n, c), f_all, hd)
        if b == 1:
            mem = proj_mem_attend(h, w_in, c, mem_k[l][0], mem_v[l][0], heads)
        else:
            (qm,) = mm(h, w_in, [BF16], cols=(c, d))
            mem = mem_attend(qm.reshape(b, n, d - c), mem_k[l], mem_v[l], heads)
        x2, (h,) = mm_resid([main.reshape(m, c), mem.reshape(m, d - c)], w_out, x2, g[3], 1.0, [g[4]])
        a = ffn_up(h, w["wg"][l][1], w["wu"][l][1], w["d_ff"])
        g_next = []
        if l + 1 < depth:
            g_next.append(w["norm_g"][l + 1, 0])
        if l == n_a - 1:
            g_next.append(w["g_kv"])
        x2, hs = mm_resid([a], w["wd"][l][1], x2, g[5], 0.5, g_next)
        if l + 1 < depth:
            h = hs[0]
        if l == n_a - 1:
            hk = hs[-1]
            k_new, k_bf = mm(hk, w["w_kv"], [F32, BF16], cols=(0, c))
            v_new, v_bf = mm(hk, w["w_kv"], [F32, BF16], cols=(c, 2 * c))
            logf_new = log_forget(hk, w["w_f"], w["b_f"]).reshape(b, n, -1)
            logf_all = logf_new if past is None else jnp.concatenate(
                [past[2].astype(F32), logf_new], axis=1)
            f_all = cumsum_time(logf_all)
    nh = c // hd
    return (x2.reshape(b, n, d), jnp.stack(new_pool), k_new.reshape(b, n, nh, hd),
            v_new.reshape(b, n, nh, hd), logf_new)


def _cast_pad_kernel(x_ref, o_ref, *, rows, cols):
    tr, tc = o_ref.shape
    r = pl.program_id(0) * tr + lax.broadcasted_iota(jnp.int32, (tr, tc), 0)
    c = pl.program_id(1) * tc + lax.broadcasted_iota(jnp.int32, (tr, tc), 1)
    o_ref[...] = jnp.where((r < rows) & (c < cols), x_ref[...], 0.0).astype(o_ref.dtype)


def _pad_ff(w4, l, s, axis):
    rows, cols = w4.shape[2:]
    shape = [rows, cols]
    shape[axis] += -shape[axis] % FF_ALIGN
    tr = _tile(shape[0], CAST_TILE_ELEMS // MM_TILE)
    tc = _tile(shape[1], CAST_TILE_ELEMS // tr)
    return pl.pallas_call(
        functools.partial(_cast_pad_kernel, rows=rows, cols=cols),
        out_shape=jax.ShapeDtypeStruct(tuple(shape), BF16),
        grid=(shape[0] // tr, shape[1] // tc),
        in_specs=[pl.BlockSpec((None, None, tr, tc), lambda i, j: (l, s, i, j))],
        out_specs=pl.BlockSpec((tr, tc), lambda i, j: (i, j)),
        compiler_params=_params(("parallel", "parallel"), 2 * tr * tc * 6 + 3 * tr * tc * 4),
        name="cast_pad",
    )(w4)


def kernel(x_prompt, x_sample, mem_prompt, cache_fox_k, cache_fox_v, cache_fox_logf, cache_mem_k, cache_mem_v, state_pool, norm_g, w_ffn_gate, w_ffn_up, w_ffn_down, w_in_a, pool_w, pool_scale, w_out_a, w_in_b, w_out_b, g_kv, w_kv, w_f, b_f, g_mem, w_mem_kv):
    depth = norm_g.shape[0]
    n_a, db, _, c = state_pool.shape
    hd = cache_fox_k.shape[3]
    _, _, n_mem, heads, mem_hd = cache_mem_k.shape
    dq = heads * mem_hd
    bp, _, d = x_prompt.shape
    w = dict(
        depth=depth, n_a=n_a, d_pool=c, hd_fox=hd, mem_heads=heads, d_ff=w_ffn_gate.shape[3],
        norm_g=norm_g, g_kv=g_kv, w_f=w_f, b_f=b_f, pool_scale=pool_scale,
        wg=[[_pad_ff(w_ffn_gate, l, s, 1) for s in range(2)] for l in range(depth)],
        wu=[[_pad_ff(w_ffn_up, l, s, 1) for s in range(2)] for l in range(depth)],
        wd=[[_pad_ff(w_ffn_down, l, s, 0) for s in range(2)] for l in range(depth)],
        w_in_a=w_in_a.astype(BF16), w_out_a=w_out_a.astype(BF16),
        w_in_b=w_in_b.astype(BF16), w_out_b=w_out_b.astype(BF16),
        pool_w=pool_w.astype(BF16), w_kv=w_kv.astype(BF16),
    )
    mem2 = mem_prompt.reshape(bp * n_mem, d)
    mem_kv = [mm(rms_cast(mem2, g_mem[l]), w_mem_kv[l].astype(BF16), [F32])[0] for l in range(depth)]
    mem_k_p = jnp.stack([kv[:, :dq].reshape(bp, n_mem, dq) for kv in mem_kv])
    mem_v_p = jnp.stack([kv[:, dq:].reshape(bp, n_mem, dq) for kv in mem_kv])

    zero_prev = jnp.zeros((n_a, bp, HALO, c), F32)
    y_p, pool_p, k_p, v_p, logf_p = _trunk(x_prompt, zero_prev, mem_k_p, mem_v_p, None, w)

    prev_s = jnp.pad(state_pool.astype(F32), ((0, 0), (0, 0), (1, 0), (0, 0)))
    past = (cache_fox_k, cache_fox_v, cache_fox_logf)
    y_s, pool_s, k_s, v_s, logf_s = _trunk(
        x_sample, prev_s, cache_mem_k.reshape(depth, db, n_mem, dq),
        cache_mem_v.reshape(depth, db, n_mem, dq), past, w)

    return (y_p, y_s, k_p, v_p, logf_p,
            mem_k_p.reshape(depth, bp, n_mem, heads, mem_hd),
            mem_v_p.reshape(depth, bp, n_mem, heads, mem_hd),
            pool_p, k_s, v_s, logf_s, pool_s)
```
